```python
import math
import jax, jax.numpy as jnp
from jax import lax
import numpy as np

D_MODEL = 2048
BATCH = 4
SEQ = 8192
DEPTH = 4

CHUNK = 64
N_META = 16
Q_BLOCK = 128
HA = 4
A_NOPE = 128
A_ROPE = 64
A_V = 128
Q_LORA = 512
KV_LORA = 256
ROPE_THETA = 10000.0
HB = 4
B_DH = 128
HC = 8
C_DH = 128
IDX_H = 8
IDX_DH = 64
TOPK_MAX = 256
N_BUCKETS = 32
MAX_DIST = 128
N_EXPERTS = 32
TOP_K = 4
D_FF = 512
SWIGLU_LIMIT = 7.0
SWIGLU_ALPHA = 1.702
MOE_BLOCK = 256
N_EVEN = (DEPTH + 1) // 2
N_ODD = DEPTH // 2
DN_ALPHA = (2 * DEPTH) ** 0.25
DN_BETA = (8 * DEPTH) ** -0.25
LN_EPS = 1e-5
RMS_EPS = 1e-6
NEG_INF = -1e30
EV_SPLITS = (Q_LORA, KV_LORA, A_ROPE, HB * B_DH, HB * B_DH, HB * B_DH)
EV_IN = sum(EV_SPLITS)
OD_SPLITS = (HC * C_DH, C_DH, C_DH, IDX_H * IDX_DH, IDX_DH, IDX_H)
OD_IN = sum(OD_SPLITS)
EV_OUT = HA * A_V + HB * B_DH
OD_OUT = HC * C_DH

kernel_name = "hybrid_mla_stickbreak_dsa_moe_trunk"


def layer_norm(x, g, b):
    xf = x.astype(jnp.float32)
    mu = jnp.mean(xf, axis=-1, keepdims=True)
    var = jnp.mean(jnp.square(xf - mu), axis=-1, keepdims=True)
    return ((xf - mu) * lax.rsqrt(var + LN_EPS) * g + b).astype(x.dtype)


def rms_norm(x, g):
    xf = x.astype(jnp.float32)
    return (xf * lax.rsqrt(jnp.mean(xf * xf, axis=-1, keepdims=True) + RMS_EPS) * g).astype(x.dtype)


def split_cols(h, sizes):
    return jnp.split(h, np.cumsum(sizes)[:-1].tolist(), axis=-1)


def chunk_ids(n):
    t = np.arange(n)
    return np.where(t < N_META, 0, (t - N_META) // CHUNK + 1)


def rope(x, pos):
    half = x.shape[-1] // 2
    inv = ROPE_THETA ** (-jnp.arange(half, dtype=jnp.float32) / half)
    ang = pos.astype(jnp.float32)[:, None] * inv[None, :]
    cos = jnp.cos(ang)[None, :, None, :]
    sin = jnp.sin(ang)[None, :, None, :]
    x1, x2 = x[..., :half], x[..., half:]
    return jnp.concatenate([x1 * cos - x2 * sin, x1 * sin + x2 * cos], axis=-1).astype(x.dtype)


def t5_bucket(rel):
    half = N_BUCKETS // 2
    max_exact = half // 2
    base = jnp.where(rel > 0, half, 0)
    n = jnp.abs(rel)
    nf = jnp.maximum(n, 1).astype(jnp.float32)
    large = max_exact + (jnp.log(nf / max_exact) / math.log(MAX_DIST / max_exact)
                         * (half - max_exact)).astype(jnp.int32)
    large = jnp.minimum(large, half - 1)
    return base + jnp.where(n < max_exact, n, large)


def sweep_query_blocks(fn, n, extend):
    outs = []
    for b in range(n // Q_BLOCK):
        start = b * Q_BLOCK
        outs.append(fn(start, min(n, start + Q_BLOCK + extend)))
    return jnp.concatenate(outs, axis=1)


def mla_attention(q, k, v, cid):
    n = q.shape[1]
    scale = (A_NOPE + A_ROPE) ** -0.5

    def block(start, kend):
        qb = q[:, start:start + Q_BLOCK]
        s = jnp.einsum('bqhd,bkhd->bhqk', qb, k[:, :kend]).astype(jnp.float32) * scale
        mask = cid[None, :kend] <= cid[start:start + Q_BLOCK, None]
        p = jax.nn.softmax(jnp.where(mask, s, NEG_INF), axis=-1)
        return jnp.einsum('bhqk,bkhd->bqhd', p.astype(v.dtype), v[:, :kend])

    return sweep_query_blocks(block, n, CHUNK)


def stick_breaking_attention(q, k, v):
    n = q.shape[1]
    scale = B_DH ** -0.5

    def block(start, kend):
        qb = q[:, start:start + Q_BLOCK]
        z = jnp.einsum('bqhd,bkhd->bhqk', qb, k[:, :kend]).astype(jnp.float32) * scale
        mask = np.arange(kend)[None, :] < (start + np.arange(Q_BLOCK))[:, None]
        log_keep = jnp.where(mask, jax.nn.log_sigmoid(-z), 0.0)
        cum = lax.cumsum(log_keep, axis=3, reverse=True)
        a = jnp.where(mask, jnp.exp(z + cum), 0.0)
        return jnp.einsum('bhqk,bkhd->bqhd', a.astype(v.dtype), v[:, :kend])

    return sweep_query_blocks(block, n, 0)


def indexer_sparse_attention(q, k, v, q_idx, k_idx, w_idx, cid, rel_table, k_sel):
    n = q.shape[1]
    cid_j = jnp.asarray(cid)

    def block(start, kend):
        qb = q[:, start:start + Q_BLOCK]
        qib = q_idx[:, start:start + Q_BLOCK]
        wb = w_idx[:, start:start + Q_BLOCK]
        cq = cid[start:start + Q_BLOCK]
        pq = start + np.arange(Q_BLOCK)
        rel_s = jax.nn.relu(jnp.einsum('bqhd,bkd->bqhk', qib, k_idx[:, :kend]).astype(jnp.float32)
                            * IDX_DH ** -0.5)
        score = jnp.einsum('bqh,bqhk->bqk', wb.astype(jnp.float32) * IDX_H ** -0.5, rel_s)
        admissible = cid[None, :kend] <= cq[:, None]
        score = jnp.where(admissible[None], score, NEG_INF)
        _, sel = lax.top_k(score, min(k_sel, kend))
        valid = cid_j[sel] <= cq[None, :, None]
        kg = jax.vmap(lambda kk, ii: kk[ii])(k, sel)
        vg = jax.vmap(lambda vv, ii: vv[ii])(v, sel)
        s = jnp.einsum('bqhd,bqkd->bqhk', qb, kg).astype(jnp.float32) * C_DH ** -0.5
        bias = rel_table[t5_bucket(sel - pq[None, :, None])].astype(jnp.float32)
        s = jnp.where(valid[:, :, None, :], s + jnp.swapaxes(bias, 2, 3), NEG_INF)
        p = jax.nn.softmax(s, axis=-1)
        return jnp.einsum('bqhk,bqkd->bqhd', p.astype(vg.dtype), vg)

    return sweep_query_blocks(block, n, CHUNK)


def even_mixer(h, pos, cid, w_in, q_norm, kv_norm, w_uq, w_ukv, w_out):
    bsz, n, _ = h.shape
    q_lat, kv_lat, k_rope, sb_q, sb_k, sb_v = split_cols(h @ w_in, EV_SPLITS)
    qa = (rms_norm(q_lat, q_norm) @ w_uq).reshape(bsz, n, HA, A_NOPE + A_ROPE)
    kva = (rms_norm(kv_lat, kv_norm) @ w_ukv).reshape(bsz, n, HA, A_NOPE + A_V)
    k_nope, v_a = kva[..., :A_NOPE], kva[..., A_NOPE:]
    k_r = jnp.broadcast_to(rope(k_rope[:, :, None, :], pos), (bsz, n, HA, A_ROPE))
    qa = jnp.concatenate([qa[..., :A_NOPE], rope(qa[..., A_NOPE:], pos)], axis=-1)
    ka = jnp.concatenate([k_nope, k_r], axis=-1)
    out_a = mla_attention(qa, ka, v_a, cid)
    shp = (bsz, n, HB, B_DH)
    out_b = stick_breaking_attention(sb_q.reshape(shp), sb_k.reshape(shp), sb_v.reshape(shp))
    merged = jnp.concatenate([out_a.reshape(bsz, n, HA * A_V), out_b.reshape(bsz, n, HB * B_DH)], axis=-1)
    return merged @ w_out


def odd_mixer(h, cid, w_in, w_out, rel_table, k_sel):
    bsz, n, _ = h.shape
    c_q, c_k, c_v, i_q, i_k, i_w = split_cols(h @ w_in, OD_SPLITS)
    out = indexer_sparse_attention(
        c_q.reshape(bsz, n, HC, C_DH), c_k, c_v,
        i_q.reshape(bsz, n, IDX_H, IDX_DH), i_k, i_w, cid, rel_table, k_sel)
    return out.reshape(bsz, n, HC * C_DH) @ w_out


def moe(h, router_w, router_b, w_in, b_in, w_out, b_out):
    bsz, n, d = h.shape
    n_tok = bsz * n
    t = h.reshape(n_tok, d)
    logits = (t @ router_w).astype(jnp.float32) + router_b
    top_val, top_idx = lax.top_k(logits, TOP_K)
    gates = jax.nn.softmax(top_val, axis=-1)
    n_assign = n_tok * TOP_K
    e_flat = top_idx.reshape(-1)
    order = jnp.argsort(e_flat)
    e_sorted = e_flat[order]
    counts = jnp.zeros((N_EXPERTS,), jnp.int32).at[e_flat].add(1)
    padded = (counts + MOE_BLOCK - 1) // MOE_BLOCK * MOE_BLOCK
    padded_end = jnp.cumsum(padded)
    start_sorted = jnp.cumsum(counts) - counts
    start_padded = padded_end - padded
    dest_sorted = start_padded[e_sorted] + jnp.arange(n_assign) - start_sorted[e_sorted]
    n_blocks = -(-n_assign // MOE_BLOCK) + N_EXPERTS
    row_tok = jnp.full((n_blocks * MOE_BLOCK,), n_tok, jnp.int32).at[dest_sorted].set(
        (order // TOP_K).astype(jnp.int32))
    block_e = jnp.minimum(jnp.searchsorted(padded_end, jnp.arange(n_blocks) * MOE_BLOCK, side='right'),
                          N_EXPERTS - 1)
    t_pad = jnp.concatenate([t, jnp.zeros((1, d), t.dtype)], axis=0)

    def run(args):
        rows, e = args
        hh = t_pad[rows] @ w_in[e] + b_in[e]
        gate, up = jnp.split(hh, 2, axis=-1)
        gate = jnp.minimum(gate, SWIGLU_LIMIT)
        up = jnp.clip(up, -SWIGLU_LIMIT, SWIGLU_LIMIT)
        return (gate * jax.nn.sigmoid(SWIGLU_ALPHA * gate) * (up + 1.0)) @ w_out[e] + b_out[e]

    y_rows = lax.map(run, (row_tok.reshape(n_blocks, MOE_BLOCK), block_e)).reshape(n_blocks * MOE_BLOCK, d)
    dest = jnp.zeros((n_assign,), jnp.int32).at[order].set(dest_sorted.astype(jnp.int32))
    y = y_rows[dest].reshape(n_tok, TOP_K, d)
    out = jnp.einsum('tk,tkd->td', gates.astype(y.dtype), y)
    return out.reshape(bsz, n, d)


def setup_inputs(seed: int = 0) -> dict:
    key = jax.random.key(seed)
    ks = jax.random.split(key, 20)
    f32 = jnp.float32
    D = D_MODEL

    def nrm(k, shape, scale):
        return jax.random.normal(k, shape, f32) * scale

    ev_col_scale = jnp.concatenate([jnp.ones((EV_IN - HB * B_DH,), f32), jnp.full((HB * B_DH,), DN_BETA, f32)])
    od_col_scale = jnp.concatenate([jnp.ones((HC * C_DH + C_DH,), f32),
                                    jnp.full((C_DH,), DN_BETA, f32),
                                    jnp.ones((OD_IN - HC * C_DH - 2 * C_DH,), f32)])
    ukv_col_scale = jnp.tile(jnp.concatenate([jnp.ones((A_NOPE,), f32), jnp.full((A_V,), DN_BETA, f32)]), HA)
    return {
        "x": nrm(ks[0], (BATCH, SEQ, D), 1.0),
        "meta_tokens": nrm(ks[1], (N_META, D), 1.0),
        "rel_bias_table": nrm(ks[2], (N_BUCKETS, HC), 0.2),
        "ev_w_in": nrm(ks[3], (N_EVEN, D, EV_IN), D ** -0.5) * ev_col_scale,
        "mla_q_norm": 1.0 + nrm(ks[4], (N_EVEN, Q_LORA), 0.02),
        "mla_kv_norm": 1.0 + nrm(ks[5], (N_EVEN, KV_LORA), 0.02),
        "mla_w_uq": nrm(ks[6], (N_EVEN, Q_LORA, HA * (A_NOPE + A_ROPE)), Q_LORA ** -0.5),
        "mla_w_ukv": nrm(ks[7], (N_EVEN, KV_LORA, HA * (A_NOPE + A_V)), KV_LORA ** -0.5) * ukv_col_scale,
        "ev_w_out": nrm(ks[8], (N_EVEN, EV_OUT, D), EV_OUT ** -0.5 * DN_BETA),
        "od_w_in": nrm(ks[9], (N_ODD, D, OD_IN), D ** -0.5) * od_col_scale,
        "od_w_out": nrm(ks[10], (N_ODD, OD_OUT, D), OD_OUT ** -0.5 * DN_BETA),
        "ln_g": 1.0 + nrm(ks[11], (DEPTH, 2, D), 0.02),
        "ln_b": nrm(ks[12], (DEPTH, 2, D), 0.02),
        "router_w": nrm(ks[13], (DEPTH, D, N_EXPERTS), D ** -0.5),
        "router_b": nrm(ks[14], (DEPTH, N_EXPERTS), 0.01),
        "exp_w_in": nrm(ks[15], (DEPTH, N_EXPERTS, D, 2 * D_FF), D ** -0.5),
        "exp_b_in": nrm(ks[16], (DEPTH, N_EXPERTS, 2 * D_FF), 0.01),
        "exp_w_out": nrm(ks[17], (DEPTH, N_EXPERTS, D_FF, D), D_FF ** -0.5 * DN_BETA),
        "exp_b_out": nrm(ks[18], (DEPTH, N_EXPERTS, D), 0.01),
    }


def reference(x, meta_tokens, rel_bias_table, ev_w_in, mla_q_norm, mla_kv_norm, mla_w_uq, mla_w_ukv,
              ev_w_out, od_w_in, od_w_out, ln_g, ln_b, router_w, router_b, exp_w_in, exp_b_in,
              exp_w_out, exp_b_out):
    bsz, seq, d = x.shape
    n_tot = seq + N_META
    n = -(-n_tot // Q_BLOCK) * Q_BLOCK
    h = jnp.concatenate([
        jnp.broadcast_to(meta_tokens[None].astype(x.dtype), (bsz, N_META, d)),
        x,
        jnp.zeros((bsz, n - n_tot, d), x.dtype)], axis=1)
    pos = jnp.arange(n)
    cid = chunk_ids(n)
    k_sel = min(TOPK_MAX, seq // 4)
    for i in range(DEPTH):
        j = i // 2
        if i % 2 == 0:
            m = even_mixer(h, pos, cid, ev_w_in[j], mla_q_norm[j], mla_kv_norm[j],
                           mla_w_uq[j], mla_w_ukv[j], ev_w_out[j])
        else:
            m = odd_mixer(h, cid, od_w_in[j], od_w_out[j], rel_bias_table, k_sel)
        h = layer_norm(DN_ALPHA * h + m, ln_g[i, 0], ln_b[i, 0])
        f = moe(h, router_w[i], router_b[i], exp_w_in[i], exp_b_in[i], exp_w_out[i], exp_b_out[i])
        h = layer_norm(DN_ALPHA * h + f, ln_g[i, 1], ln_b[i, 1])
    return h[:, N_META:N_META + seq]
```

```python
import functools
import math

import numpy as np
import jax
import jax.numpy as jnp
from jax import lax
from jax.experimental import pallas as pl
from jax.experimental.pallas import tpu as pltpu

f32 = jnp.float32
bf16 = jnp.bfloat16
i32 = jnp.int32

CHUNK = 64
N_META = 16
Q_BLOCK = 128
HA, A_NOPE, A_ROPE, A_V = 4, 128, 64, 128
Q_LORA, KV_LORA = 512, 256
ROPE_THETA = 10000.0
HB, B_DH = 4, 128
HC, C_DH = 8, 128
IDX_H, IDX_DH = 8, 64
TOPK_MAX = 256
N_BUCKETS, MAX_DIST = 32, 128
N_EXPERTS, TOP_K, D_FF = 32, 4, 512
SWIGLU_LIMIT, SWIGLU_ALPHA = 7.0, 1.702
LN_EPS, RMS_EPS = 1e-5, 1e-6

LANE = 128
SUBLANE = 8
VMEM_LIMIT = 56 * 1024 * 1024

NEG = -1e30
INT_MIN = -(2 ** 31)
CID_SHIFT = CHUNK - N_META
CHUNK_LOG2 = 6
MOE_ROWS = 256
TOK_BLK = 256
SB_EXIT = -110.0


def _cparams(sem):
    return pltpu.CompilerParams(dimension_semantics=sem, vmem_limit_bytes=VMEM_LIMIT)


def _pick(n, cands):
    for c in cands:
        if n % c == 0:
            return c
    raise ValueError(f"no block size for {n}")


def _nt(a, b):
    return lax.dot_general(a, b, (((1,), (1,)), ((), ())), preferred_element_type=f32)


def _rms(x, g):
    return x * lax.rsqrt(jnp.mean(x * x, axis=-1, keepdims=True) + RMS_EPS) * g


EV_N = Q_LORA + KV_LORA + 3 * HB * B_DH + 2 * A_ROPE


def _rope_pair(block, tab, keep_low_only):
    prod = block * tab
    r = prod + pltpu.roll(prod, A_ROPE, 1)
    if keep_low_only:
        lane = lax.broadcasted_iota(i32, r.shape, 1)
        r = jnp.where(lane < A_ROPE, r, 0.0)
    return r


def _even_proj_kernel(x_ref, w_ref, qn_ref, kvn_ref, wuq_ref, wukv_ref, tab_ref,
                      qa_ref, ka_ref, va_ref, sbq_ref, sbk_ref, sbv_ref):
    acc = jnp.dot(x_ref[...], w_ref[...], preferred_element_type=f32)
    o = Q_LORA + KV_LORA
    hb = HB * B_DH
    sbq_ref[...] = (acc[:, o:o + hb] * (B_DH ** -0.5)).astype(bf16)
    sbk_ref[...] = acc[:, o + hb:o + 2 * hb].astype(bf16)
    sbv_ref[...] = acc[:, o + 2 * hb:o + 3 * hb].astype(bf16)
    tab = tab_ref[...]
    k_rope = _rope_pair(acc[:, o + 3 * hb:o + 3 * hb + 2 * A_ROPE], tab, False).astype(bf16)
    qn = _rms(acc[:, :Q_LORA], qn_ref[...]).astype(bf16)
    kvn = _rms(acc[:, Q_LORA:o], kvn_ref[...]).astype(bf16)
    qraw = jnp.dot(qn, wuq_ref[...], preferred_element_type=f32)
    kva = jnp.dot(kvn, wukv_ref[...], preferred_element_type=f32)
    scale = (A_NOPE + A_ROPE) ** -0.5
    for h in range(HA):
        b0 = h * 2 * LANE
        qa_ref[:, b0:b0 + LANE] = (qraw[:, b0:b0 + LANE] * scale).astype(bf16)
        qr = _rope_pair(qraw[:, b0 + LANE:b0 + 2 * LANE], tab, True)
        qa_ref[:, b0 + LANE:b0 + 2 * LANE] = (qr * scale).astype(bf16)
        ka_ref[:, b0:b0 + LANE] = kva[:, h * A_NOPE:(h + 1) * A_NOPE].astype(bf16)
        ka_ref[:, b0 + LANE:b0 + 2 * LANE] = k_rope
    va_ref[...] = kva[:, HA * A_NOPE:].astype(bf16)


def _even_proj(hb, w_in, q_norm, kv_norm, w_uq, w_ukv, tab, n):
    m, d = hb.shape
    tm = _pick(n, (640, 512, 256, 128))
    nb = n // tm
    const = lambda i: (0, 0)
    row = lambda i: (i, 0)
    outs = [jax.ShapeDtypeStruct((m, c), bf16) for c in
            (HA * 2 * LANE, HA * 2 * LANE, HA * A_V, HB * B_DH, HB * B_DH, HB * B_DH)]
    return pl.pallas_call(
        _even_proj_kernel,
        out_shape=outs,
        grid=(m // tm,),
        in_specs=[pl.BlockSpec((tm, d), row),
                  pl.BlockSpec(w_in.shape, const),
                  pl.BlockSpec(q_norm.shape, const),
                  pl.BlockSpec(kv_norm.shape, const),
                  pl.BlockSpec(w_uq.shape, const),
                  pl.BlockSpec(w_ukv.shape, const),
                  pl.BlockSpec((tm, LANE), lambda i: (i % nb, 0))],
        out_specs=[pl.BlockSpec((tm, s.shape[1]), row) for s in outs],
        compiler_params=_cparams(("parallel",)),
        name="even_proj",
    )(hb, w_in, q_norm, kv_norm, w_uq, w_ukv, tab)


def _mla_kernel(q_ref, k_ref, v_ref, o_ref, m_ref, l_ref, acc_ref, *, t, nblk):
    qi = pl.program_id(2)
    q = q_ref[...]
    m_ref[...] = jnp.full(m_ref.shape, NEG, f32)
    l_ref[...] = jnp.zeros(l_ref.shape, f32)
    acc_ref[...] = jnp.zeros(acc_ref.shape, f32)

    def step(j, masked):
        off = pl.multiple_of(j * t, t)
        s = _nt(q, k_ref[pl.ds(off, t), :])
        if masked:
            qpos = qi * t + lax.broadcasted_iota(i32, (t, t), 0)
            kpos = off + lax.broadcasted_iota(i32, (t, t), 1)
            vis = ((kpos + CID_SHIFT) >> CHUNK_LOG2) <= ((qpos + CID_SHIFT) >> CHUNK_LOG2)
            s = jnp.where(vis, s, NEG)
        m_old = m_ref[...]
        m_new = jnp.maximum(m_old, jnp.max(s, axis=1, keepdims=True))
        alpha = jnp.exp(m_old - m_new)
        p = jnp.exp(s - m_new)
        l_ref[...] = alpha * l_ref[...] + jnp.sum(p, axis=1, keepdims=True)
        acc_ref[...] = alpha * acc_ref[...] + jnp.dot(p.astype(bf16), v_ref[pl.ds(off, t), :],
                                                      preferred_element_type=f32)
        m_ref[...] = m_new

    def full_body(j, c):
        step(j, False)
        return c

    def diag_body(j, c):
        step(j, True)
        return c

    lax.fori_loop(0, qi, full_body, 0)
    lax.fori_loop(qi, jnp.minimum(qi + 2, nblk), diag_body, 0)
    o_ref[...] = (acc_ref[...] / l_ref[...]).astype(bf16)


def _mla_attention(qa, ka, va, bsz, n):
    m = qa.shape[0]
    t = _pick(n, (640, 512, 256, 128))
    nblk = n // t
    return pl.pallas_call(
        functools.partial(_mla_kernel, t=t, nblk=nblk),
        out_shape=jax.ShapeDtypeStruct((m, HA * A_V), bf16),
        grid=(bsz, HA, nblk),
        in_specs=[pl.BlockSpec((t, 2 * LANE), lambda b, h, i: (b * nblk + i, h)),
                  pl.BlockSpec((n, 2 * LANE), lambda b, h, i: (b, h)),
                  pl.BlockSpec((n, A_V), lambda b, h, i: (b, h))],
        out_specs=pl.BlockSpec((t, A_V), lambda b, h, i: (b * nblk + i, h)),
        scratch_shapes=[pltpu.VMEM((t, 1), f32), pltpu.VMEM((t, 1), f32), pltpu.VMEM((t, A_V), f32)],
        compiler_params=_cparams(("parallel", "parallel", "parallel")),
        name="mla_attention",
    )(qa, ka, va)


SB_T = 128


def _sb_kernel(q_ref, k_ref, v_ref, u_ref, o_ref):
    t = SB_T
    qi = pl.program_id(2)
    q = q_ref[...]
    u = u_ref[...]

    def block(j, c, acc, masked):
        off = pl.multiple_of(j * t, t)
        z = _nt(q, k_ref[pl.ds(off, t), :])
        lk = -(jnp.maximum(z, 0.0) + jnp.log(1.0 + jnp.exp(-jnp.abs(z))))
        if masked:
            row = lax.broadcasted_iota(i32, (t, t), 0)
            col = lax.broadcasted_iota(i32, (t, t), 1)
            causal = col < row
            lk = jnp.where(causal, lk, 0.0)
        hi = lk.astype(bf16)
        lo = (lk - hi.astype(f32)).astype(bf16)
        cum = (jnp.dot(hi, u, preferred_element_type=f32) + jnp.dot(lo, u, preferred_element_type=f32))
        a = jnp.exp(z + c + cum)
        if masked:
            a = jnp.where(causal, a, 0.0)
        acc = acc + jnp.dot(a.astype(bf16), v_ref[pl.ds(off, t), :], preferred_element_type=f32)
        return c + cum[:, 0:1], acc

    c, acc = block(qi, jnp.zeros((t, 1), f32), jnp.zeros((t, B_DH), f32), True)

    def cond(st):
        j, c, _ = st
        return jnp.logical_and(j >= 0, jnp.max(c) > SB_EXIT)

    def body(st):
        j, c, acc = st
        c, acc = block(j, c, acc, False)
        return j - 1, c, acc

    _, _, acc = lax.while_loop(cond, body, (qi - 1, c, acc))
    o_ref[...] = acc.astype(bf16)


def _sb_attention(sbq, sbk, sbv, bsz, n):
    m = sbq.shape[0]
    t = SB_T
    nblk = n // t
    u = (np.arange(t)[:, None] >= np.arange(t)[None, :]).astype(np.float32)
    u = jnp.asarray(u, bf16)
    return pl.pallas_call(
        _sb_kernel,
        out_shape=jax.ShapeDtypeStruct((m, HB * B_DH), bf16),
        grid=(bsz, HB, nblk),
        in_specs=[pl.BlockSpec((t, B_DH), lambda b, h, i: (b * nblk + i, h)),
                  pl.BlockSpec((n, B_DH), lambda b, h, i: (b, h)),
                  pl.BlockSpec((n, B_DH), lambda b, h, i: (b, h)),
                  pl.BlockSpec((t, t), lambda b, h, i: (0, 0))],
        out_specs=pl.BlockSpec((t, B_DH), lambda b, h, i: (b * nblk + i, h)),
        compiler_params=_cparams(("parallel", "parallel", "parallel")),
        name="stick_breaking_attention",
    )(sbq, sbk, sbv, u)


def _pack_bf16_pairs(x):
    half = x.shape[1] // 2
    xb = x.astype(bf16).astype(f32)
    lo = lax.bitcast_convert_type(xb[:, :half], i32)
    hi = lax.bitcast_convert_type(xb[:, half:], i32)
    return ((lo >> 16) & 0xFFFF) | (hi & (-65536))


def _unpack_bf16_pairs(w):
    lo = lax.bitcast_convert_type(w << 16, f32).astype(bf16)
    hi = lax.bitcast_convert_type(w & (-65536), f32).astype(bf16)
    return jnp.concatenate([lo, hi], axis=1)


def _layer_norm(y, g, b):
    mu = jnp.mean(y, axis=-1, keepdims=True)
    yc = y - mu
    var = jnp.mean(yc * yc, axis=-1, keepdims=True)
    return yc * lax.rsqrt(var + LN_EPS) * g + b


def _mix_out_kernel(*refs, nparts, alpha):
    a_refs = refs[:nparts]
    w_refs = refs[nparts:2 * nparts]
    h_ref, g_ref, b_ref, rwh_ref, rwl_ref, rb_ref = refs[2 * nparts:2 * nparts + 6]
    h1_ref, hp_ref, gate_ref, idx_ref = refs[2 * nparts + 6:]
    mix = jnp.dot(a_refs[0][...], w_refs[0][...], preferred_element_type=f32)
    for a_ref, w_ref in zip(a_refs[1:], w_refs[1:]):
        mix = mix + jnp.dot(a_ref[...], w_ref[...], preferred_element_type=f32)
    hn = _layer_norm(alpha * h_ref[...] + mix, g_ref[...], b_ref[...])
    h1_ref[...] = hn
    hp_ref[...] = _pack_bf16_pairs(hn)
    x_hi = hn.astype(bf16)
    x_lo = (hn - x_hi.astype(f32)).astype(bf16)
    rwh = rwh_ref[...]
    logits = (jnp.dot(x_hi, rwh, preferred_element_type=f32) + jnp.dot(x_lo, rwh, preferred_element_type=f32)
              + jnp.dot(x_hi, rwl_ref[...], preferred_element_type=f32) + rb_ref[...])
    lane_i = lax.broadcasted_iota(i32, logits.shape, 1)
    lane = lane_i.astype(f32)
    cur = jnp.where(lane_i < N_EXPERTS, logits, -jnp.inf)
    vals, idxs = [], []
    for _ in range(TOP_K):
        mx = jnp.max(cur, axis=1, keepdims=True)
        ix = jnp.min(jnp.where(cur == mx, lane, float(LANE)), axis=1, keepdims=True)
        vals.append(mx)
        idxs.append(ix)
        cur = jnp.where(lane == ix, -jnp.inf, cur)
    es = [jnp.exp(v - vals[0]) for v in vals]
    den = es[0]
    for e in es[1:]:
        den = den + e
    gate = jnp.zeros(logits.shape, f32)
    idx = jnp.zeros(logits.shape, f32)
    for k in range(TOP_K):
        gate = jnp.where(lane_i == k, es[k] / den, gate)
        idx = jnp.where(lane_i == k, idxs[k], idx)
    gate_ref[...] = gate
    idx_ref[...] = idx.astype(i32)


def _mix_out(parts, weights, h, g, b, rwh, rwl, rb, alpha):
    m, d = h.shape
    tm = _pick(m, (256, 128))
    row = lambda i: (i, 0)
    const = lambda i: (0, 0)
    nparts = len(parts)
    outs = [jax.ShapeDtypeStruct((m, d), f32), jax.ShapeDtypeStruct((m, d // 2), i32),
            jax.ShapeDtypeStruct((m, LANE), f32), jax.ShapeDtypeStruct((m, LANE), i32)]
    in_specs = ([pl.BlockSpec((tm, p.shape[1]), row) for p in parts]
                + [pl.BlockSpec(w.shape, const) for w in weights]
                + [pl.BlockSpec((tm, d), row), pl.BlockSpec((1, d), const), pl.BlockSpec((1, d), const),
                   pl.BlockSpec(rwh.shape, const), pl.BlockSpec(rwl.shape, const), pl.BlockSpec((1, LANE), const)])
    return pl.pallas_call(
        functools.partial(_mix_out_kernel, nparts=nparts, alpha=alpha),
        out_shape=outs,
        grid=(m // tm,),
        in_specs=in_specs,
        out_specs=[pl.BlockSpec((tm, s.shape[1]), row) for s in outs],
        compiler_params=_cparams(("parallel",)),
        name="mix_out_ln_router",
    )(*parts, *weights, h, g, b, rwh, rwl, rb)


def _row_copy(src, dst, sem):
    return pltpu.make_async_copy(src, dst, sem)


def _dispatch_kernel(dest_hbm, hp_ref, xs_in, xs_out, dest_smem, sem_idx, sem_rows):
    del xs_in
    i = pl.program_id(0)
    cp = pltpu.make_async_copy(dest_hbm.at[i], dest_smem, sem_idx)
    cp.start()
    cp.wait()
    tm = hp_ref.shape[0]

    def issue(r, c):
        for k in range(TOP_K):
            d = dest_smem[r * TOP_K + k]
            _row_copy(hp_ref.at[pl.ds(r, 1)], xs_out.at[pl.ds(d, 1)], sem_rows).start()
        return c

    lax.fori_loop(0, tm, issue, 0)

    def drain(r, c):
        for k in range(TOP_K):
            _row_copy(hp_ref.at[pl.ds(0, 1)], xs_out.at[pl.ds(0, 1)], sem_rows).wait()
        return c

    lax.fori_loop(0, tm, drain, 0)


def _dispatch(dest2d, hp, n_rows):
    m, dh = hp.shape
    tm = TOK_BLK
    zeros = jnp.zeros((n_rows, dh), i32)
    return pl.pallas_call(
        _dispatch_kernel,
        out_shape=jax.ShapeDtypeStruct((n_rows, dh), i32),
        grid=(m // tm,),
        in_specs=[pl.BlockSpec(memory_space=pl.ANY),
                  pl.BlockSpec((tm, dh), lambda i: (i, 0)),
                  pl.BlockSpec(memory_space=pl.ANY)],
        out_specs=pl.BlockSpec(memory_space=pl.ANY),
        scratch_shapes=[pltpu.SMEM((tm * TOP_K,), i32), pltpu.SemaphoreType.DMA, pltpu.SemaphoreType.DMA],
        input_output_aliases={2: 0},
        compiler_params=_cparams(("arbitrary",)),
        name="moe_dispatch",
    )(dest2d, hp, zeros)


def _expert_kernel(be_ref, nu_ref, x_ref, wi_ref, bi_ref, wo_ref, bo_ref, y_ref):
    del be_ref

    @pl.when(pl.program_id(0) < nu_ref[0])
    def _():
        xb = _unpack_bf16_pairs(x_ref[...])
        hh = jnp.dot(xb, wi_ref[...], preferred_element_type=f32) + bi_ref[...]
        gate = jnp.minimum(hh[:, :D_FF], SWIGLU_LIMIT)
        up = jnp.clip(hh[:, D_FF:], -SWIGLU_LIMIT, SWIGLU_LIMIT)
        act = gate * (1.0 / (1.0 + jnp.exp(-SWIGLU_ALPHA * gate))) * (up + 1.0)
        y_ref[...] = jnp.dot(act.astype(bf16), wo_ref[...], preferred_element_type=f32) + bo_ref[...]


def _experts(block_e, n_used, xs, w_in, b_in, w_out, b_out):
    n_rows, dh = xs.shape
    d = 2 * dh
    nb = n_rows // MOE_ROWS
    blk = lambda i, be, nu: (jnp.minimum(i, nu[0] - 1), 0)
    per_e = lambda i, be, nu: (be[i], 0, 0)
    grid_spec = pltpu.PrefetchScalarGridSpec(
        num_scalar_prefetch=2,
        grid=(nb,),
        in_specs=[pl.BlockSpec((MOE_ROWS, dh), blk),
                  pl.BlockSpec((None, d, 2 * D_FF), per_e),
                  pl.BlockSpec((None, 1, 2 * D_FF), per_e),
                  pl.BlockSpec((None, D_FF, d), per_e),
                  pl.BlockSpec((None, 1, d), per_e)],
        out_specs=pl.BlockSpec((MOE_ROWS, d), blk),
    )
    return pl.pallas_call(
        _expert_kernel,
        out_shape=jax.ShapeDtypeStruct((n_rows, d), f32),
        grid_spec=grid_spec,
        compiler_params=_cparams(("arbitrary",)),
        name="moe_experts",
    )(block_e, n_used, xs, w_in, b_in, w_out, b_out)


def _combine_kernel(dest_hbm, y_hbm, gate_ref, h_ref, g_ref, b_ref, h2_ref, h2b_ref,
                    dest_smem, ybuf, sem_idx, sem_rows, *, alpha):
    i = pl.program_id(0)
    cp = pltpu.make_async_copy(dest_hbm.at[i], dest_smem, sem_idx)
    cp.start()
    cp.wait()
    tm = h_ref.shape[0]

    def issue(r, c):
        for k in range(TOP_K):
            d = dest_smem[r * TOP_K + k]
            _row_copy(y_hbm.at[pl.ds(d, 1)], ybuf.at[k, pl.ds(r, 1)], sem_rows).start()
        return c

    lax.fori_loop(0, tm, issue, 0)

    def drain(r, c):
        for k in range(TOP_K):
            _row_copy(y_hbm.at[pl.ds(0, 1)], ybuf.at[0, pl.ds(0, 1)], sem_rows).wait()
        return c

    lax.fori_loop(0, tm, drain, 0)
    gate = gate_ref[...]
    f = gate[:, 0:1] * ybuf[0]
    for k in range(1, TOP_K):
        f = f + gate[:, k:k + 1] * ybuf[k]
    hn = _layer_norm(alpha * h_ref[...] + f, g_ref[...], b_ref[...])
    h2_ref[...] = hn
    h2b_ref[...] = hn.astype(bf16)


def _combine(dest2d, y, gate, h, g, b, alpha):
    m, d = h.shape
    tm = TOK_BLK
    row = lambda i: (i, 0)
    const = lambda i: (0, 0)
    return pl.pallas_call(
        functools.partial(_combine_kernel, alpha=alpha),
        out_shape=[jax.ShapeDtypeStruct((m, d), f32), jax.ShapeDtypeStruct((m, d), bf16)],
        grid=(m // tm,),
        in_specs=[pl.BlockSpec(memory_space=pl.ANY),
                  pl.BlockSpec(memory_space=pl.ANY),
                  pl.BlockSpec((tm, LANE), row),
                  pl.BlockSpec((tm, d), row),
                  pl.BlockSpec((1, d), const),
                  pl.BlockSpec((1, d), const)],
        out_specs=[pl.BlockSpec((tm, d), row), pl.BlockSpec((tm, d), row)],
        scratch_shapes=[pltpu.SMEM((tm * TOP_K,), i32), pltpu.VMEM((TOP_K, tm, d), f32),
                        pltpu.SemaphoreType.DMA, pltpu.SemaphoreType.DMA],
        compiler_params=_cparams(("arbitrary",)),
        name="moe_combine_ln",
    )(dest2d, y, gate, h, g, b)


def _route(idx, m):
    e = idx[:, :TOP_K].reshape(-1)
    oh = (e[:, None] == jnp.arange(N_EXPERTS, dtype=i32)[None, :]).astype(i32)
    csum = jnp.cumsum(oh, axis=0)
    counts = csum[-1]
    padded = (counts + MOE_ROWS - 1) // MOE_ROWS * MOE_ROWS
    padded_end = jnp.cumsum(padded)
    start_padded = padded_end - padded
    dest = jnp.sum(oh * (csum - 1 + start_padded[None, :]), axis=1).astype(i32)
    n_blocks = -(-(m * TOP_K) // MOE_ROWS) + N_EXPERTS
    block_e = jnp.minimum(jnp.searchsorted(padded_end, jnp.arange(n_blocks, dtype=i32) * MOE_ROWS, side='right'),
                          N_EXPERTS - 1).astype(i32)
    n_used = (padded_end[-1:] // MOE_ROWS).astype(i32)
    return dest.reshape(m // TOK_BLK, TOK_BLK * TOP_K), block_e, n_used, n_blocks * MOE_ROWS


def _moe(hp, h1, gate, idx, w_in, b_in, w_out, b_out, g, b, alpha):
    m = h1.shape[0]
    dest2d, block_e, n_used, n_rows = _route(idx, m)
    xs = _dispatch(dest2d, hp, n_rows)
    y = _experts(block_e, n_used, xs, w_in, b_in, w_out, b_out)
    return _combine(dest2d, y, gate, h1, g, b, alpha)


OD_Q = HC * C_DH
OD_N = OD_Q + 2 * C_DH + IDX_H * LANE + LANE


def _odd_proj_kernel(x_ref, w_ref, wv_ref, ww_ref, cq_ref, ck_ref, iq_ref, ik_ref, vt_ref, wt_ref):
    x = x_ref[...]
    acc = jnp.dot(x, w_ref[...], preferred_element_type=f32)
    cq_ref[...] = (acc[:, :OD_Q] * (C_DH ** -0.5)).astype(bf16)
    ck_ref[...] = acc[:, OD_Q:OD_Q + C_DH].astype(bf16)
    o = OD_Q + C_DH
    iq_ref[...] = (acc[:, o:o + IDX_H * LANE] * (IDX_DH ** -0.5)).astype(bf16)
    ik_ref[...] = acc[:, o + IDX_H * LANE:].astype(bf16)
    vt = _nt(wv_ref[...], x).astype(bf16)
    for c in range(vt_ref.shape[0]):
        vt_ref[c] = vt[:, c * LANE:(c + 1) * LANE]
    wt_ref[...] = _nt(ww_ref[...], x)[:IDX_H] * (IDX_H ** -0.5)


def _odd_proj(hb, w_main, w_v_t, w_w_t, n):
    m, d = hb.shape
    tm = _pick(n, (640, 512, 256, 128))
    row = lambda i: (i, 0)
    const = lambda i: (0, 0)
    outs = [jax.ShapeDtypeStruct((m, OD_Q), bf16), jax.ShapeDtypeStruct((m, C_DH), bf16),
            jax.ShapeDtypeStruct((m, IDX_H * LANE), bf16), jax.ShapeDtypeStruct((m, LANE), bf16),
            jax.ShapeDtypeStruct((m // LANE, C_DH, LANE), bf16), jax.ShapeDtypeStruct((IDX_H, m), f32)]
    return pl.pallas_call(
        _odd_proj_kernel,
        out_shape=outs,
        grid=(m // tm,),
        in_specs=[pl.BlockSpec((tm, d), row), pl.BlockSpec(w_main.shape, const),
                  pl.BlockSpec(w_v_t.shape, const), pl.BlockSpec(w_w_t.shape, const)],
        out_specs=[pl.BlockSpec((tm, OD_Q), row), pl.BlockSpec((tm, C_DH), row),
                   pl.BlockSpec((tm, IDX_H * LANE), row), pl.BlockSpec((tm, LANE), row),
                   pl.BlockSpec((tm // LANE, C_DH, LANE), lambda i: (i, 0, 0)),
                   pl.BlockSpec((IDX_H, tm), lambda i: (0, i))],
        compiler_params=_cparams(("parallel",)),
        name="odd_proj",
    )(hb, w_main, w_v_t, w_w_t)


DSA_T = 128
IDX_BITS = 14


def _dsa_kernel(cq_ref, iq_ref, wt_ref, ik_ref, ck_ref, vt_ref, tile_ref, far_ref, o_ref,
                ks_ref, thr_ref, acc_ref, m_ref, l_ref, *, nblk, k_sel):
    t = DSA_T
    qi = pl.program_id(1)
    nkb = jnp.minimum(qi + 2, nblk)
    s_iota = lax.broadcasted_iota(i32, (t, t), 0)
    t_iota = lax.broadcasted_iota(i32, (t, t), 1)
    tcid = (qi * t + t_iota + CID_SHIFT) >> CHUNK_LOG2

    iq = iq_ref[...]
    iq_stack = jnp.concatenate([iq[:, h * LANE:(h + 1) * LANE] for h in range(IDX_H)], axis=0)
    wt = wt_ref[...]

    def score_block(j, masked):
        off = pl.multiple_of(j * t, t)
        st = _nt(ik_ref[pl.ds(off, t), :], iq_stack)
        sc = jnp.maximum(st[:, :t], 0.0) * wt[0:1, :]
        for h in range(1, IDX_H):
            sc = sc + jnp.maximum(st[:, h * t:(h + 1) * t], 0.0) * wt[h:h + 1, :]
        sc = jnp.where(sc == 0.0, 0.0, sc)
        bits = lax.bitcast_convert_type(sc, i32)
        key = jnp.where(bits < 0, bits ^ 0x7FFFFFFF, bits)
        if masked:
            adm = ((off + s_iota + CID_SHIFT) >> CHUNK_LOG2) <= tcid
            key = jnp.where(adm, key, INT_MIN)
        ks_ref[pl.ds(off, t), :] = key

    def p1_full(j, c):
        score_block(j, False)
        return c

    def p1_diag(j, c):
        score_block(j, True)
        return c

    lax.fori_loop(0, qi, p1_full, 0)
    lax.fori_loop(qi, nkb, p1_diag, 0)

    def count(pred_fn):
        def body(j, cnt):
            off = pl.multiple_of(j * t, t)
            hit = jnp.where(pred_fn(ks_ref[pl.ds(off, t), :], off), 1.0, 0.0)
            return cnt + jnp.sum(hit.reshape(t // SUBLANE, SUBLANE, t), axis=0)
        cnt = lax.fori_loop(0, nkb, body, jnp.zeros((SUBLANE, t), f32))
        return jnp.sum(cnt, axis=0, keepdims=True)

    def bit_step(b, thr):
        cand = thr + lax.shift_left(jnp.int32(1), 31 - b)
        cnt = count(lambda kk, off: kk >= cand)
        return jnp.where(cnt >= float(k_sel), cand, thr)

    thr = lax.fori_loop(0, 32, bit_step, jnp.full((1, t), INT_MIN, i32))
    thr = jnp.maximum(thr, INT_MIN + 1)

    n_gt = count(lambda kk, off: kk > thr)
    n_eq = count(lambda kk, off: kk == thr)
    need = float(k_sel) - n_gt
    thr_ref[0:1, :] = thr
    thr_ref[1:2, :] = jnp.full((1, t), 2 ** 30, i32)

    @pl.when(jnp.max(n_eq - need) > 0.0)
    def _():
        def idx_step(b, lim):
            cand = lim + lax.shift_left(jnp.int32(1), IDX_BITS - 1 - b)
            cnt = count(lambda kk, off: jnp.logical_and(kk == thr, (off + s_iota) < cand))
            return jnp.where(cnt <= need, cand, lim)
        thr_ref[1:2, :] = lax.fori_loop(0, IDX_BITS, idx_step, jnp.zeros((1, t), i32))

    lim = thr_ref[1:2, :]

    cq = cq_ref[...]
    q_stack = jnp.concatenate([cq[:, h * LANE:(h + 1) * LANE] for h in range(HC)], axis=0)
    m_ref[...] = jnp.full(m_ref.shape, NEG, f32)
    l_ref[...] = jnp.zeros(l_ref.shape, f32)
    acc_ref[...] = jnp.zeros(acc_ref.shape, f32)

    def attn_block(j, near):
        off = pl.multiple_of(j * t, t)
        kk = ks_ref[pl.ds(off, t), :]
        sel = kk >= jnp.where((off + s_iota) < lim, thr, thr + 1)
        st = _nt(ck_ref[pl.ds(off, t), :], q_stack)
        m_old = m_ref[...]
        ps, m_news = [], []
        for h in range(HC):
            s = st[:, h * t:(h + 1) * t]
            if near:
                s = s + tile_ref[j - qi + 1, h]
            else:
                s = s + far_ref[h:h + 1, :]
            s = jnp.where(sel, s, NEG)
            m_new = jnp.maximum(m_old[:, h * t:(h + 1) * t], jnp.max(s, axis=0, keepdims=True))
            ps.append(jnp.where(sel, jnp.exp(s - m_new), 0.0))
            m_news.append(m_new)
        p = jnp.concatenate(ps, axis=1)
        m_new = jnp.concatenate(m_news, axis=1)
        alpha = jnp.exp(m_old - m_new)
        l_ref[...] = alpha * l_ref[...] + jnp.sum(p, axis=0, keepdims=True)
        acc_ref[...] = alpha * acc_ref[...] + jnp.dot(vt_ref[j], p.astype(bf16), preferred_element_type=f32)
        m_ref[...] = m_new

    def p4_far(j, c):
        attn_block(j, False)
        return c

    def p4_near(j, c):
        attn_block(j, True)
        return c

    n_far = jnp.maximum(qi - 1, 0)
    lax.fori_loop(0, n_far, p4_far, 0)
    lax.fori_loop(n_far, nkb, p4_near, 0)
    out_t = acc_ref[...] / l_ref[...]
    for h in range(HC):
        o_ref[:, h * C_DH:(h + 1) * C_DH] = out_t[:, h * t:(h + 1) * t].T.astype(bf16)


def _dsa_attention(cq, ck, iq, ik, vt, wt, tiles, far, bsz, n, k_sel):
    m = cq.shape[0]
    t = DSA_T
    nblk = n // t
    vt4 = vt.reshape(bsz, nblk, C_DH, LANE)
    return pl.pallas_call(
        functools.partial(_dsa_kernel, nblk=nblk, k_sel=k_sel),
        out_shape=jax.ShapeDtypeStruct((m, HC * C_DH), bf16),
        grid=(bsz, nblk),
        in_specs=[pl.BlockSpec((t, HC * C_DH), lambda b, i: (b * nblk + i, 0)),
                  pl.BlockSpec((t, IDX_H * LANE), lambda b, i: (b * nblk + i, 0)),
                  pl.BlockSpec((IDX_H, t), lambda b, i: (0, b * nblk + i)),
                  pl.BlockSpec((n, LANE), lambda b, i: (b, 0)),
                  pl.BlockSpec((n, C_DH), lambda b, i: (b, 0)),
                  pl.BlockSpec((None, nblk, C_DH, LANE), lambda b, i: (b, 0, 0, 0)),
                  pl.BlockSpec(tiles.shape, lambda b, i: (0, 0, 0, 0)),
                  pl.BlockSpec(far.shape, lambda b, i: (0, 0))],
        out_specs=pl.BlockSpec((t, HC * C_DH), lambda b, i: (b * nblk + i, 0)),
        scratch_shapes=[pltpu.VMEM((n, t), i32), pltpu.VMEM((SUBLANE, t), i32),
                        pltpu.VMEM((C_DH, HC * t), f32), pltpu.VMEM((1, HC * t), f32),
                        pltpu.VMEM((1, HC * t), f32)],
        compiler_params=_cparams(("parallel", "parallel")),
        name="dsa_attention",
    )(cq, iq, wt, ik, ck, vt4, tiles, far)


def _rot_half_cols(w):
    half = w.shape[-1] // 2
    return jnp.concatenate([-w[..., half:], w[..., :half]], axis=-1)


def _prep_even(w_in, w_uq, w_ukv):
    o0 = Q_LORA + KV_LORA
    kr = w_in[:, o0:o0 + A_ROPE]
    w_in2 = jnp.concatenate([w_in[:, :o0], w_in[:, o0 + A_ROPE:], kr, _rot_half_cols(kr)], axis=1).astype(bf16)
    wq = w_uq.reshape(Q_LORA, HA, A_NOPE + A_ROPE)
    wq_r = wq[:, :, A_NOPE:]
    wq2 = jnp.concatenate([wq[:, :, :A_NOPE], wq_r, _rot_half_cols(wq_r)], axis=-1).reshape(Q_LORA, HA * 2 * LANE)
    wkv = w_ukv.reshape(KV_LORA, HA, A_NOPE + A_V)
    wkv2 = jnp.concatenate([wkv[:, :, :A_NOPE].reshape(KV_LORA, HA * A_NOPE),
                            wkv[:, :, A_NOPE:].reshape(KV_LORA, HA * A_V)], axis=1)
    return w_in2, wq2.astype(bf16), wkv2.astype(bf16)


def _prep_odd(w_in):
    d = w_in.shape[0]
    o = OD_Q
    w_q, w_k, w_v = w_in[:, :o], w_in[:, o:o + C_DH], w_in[:, o + C_DH:o + 2 * C_DH]
    o2 = o + 2 * C_DH
    w_iq = w_in[:, o2:o2 + IDX_H * IDX_DH].reshape(d, IDX_H, IDX_DH)
    w_iq = jnp.pad(w_iq, ((0, 0), (0, 0), (0, LANE - IDX_DH))).reshape(d, IDX_H * LANE)
    o3 = o2 + IDX_H * IDX_DH
    w_ik = jnp.pad(w_in[:, o3:o3 + IDX_DH], ((0, 0), (0, LANE - IDX_DH)))
    w_iw = w_in[:, o3 + IDX_DH:o3 + IDX_DH + IDX_H]
    w_main = jnp.concatenate([w_q, w_k, w_iq, w_ik], axis=1).astype(bf16)
    w_v_t = w_v.T.astype(bf16)
    w_w_t = jnp.pad(w_iw.T, ((0, 2 * SUBLANE - IDX_H), (0, 0))).astype(bf16)
    return w_main, w_v_t, w_w_t


def _t5_bucket(rel):
    half = N_BUCKETS // 2
    max_exact = half // 2
    base = jnp.where(rel > 0, half, 0)
    nn = jnp.abs(rel)
    nf = jnp.maximum(nn, 1).astype(f32)
    large = max_exact + (jnp.log(nf / max_exact) / math.log(MAX_DIST / max_exact)
                         * (half - max_exact)).astype(i32)
    large = jnp.minimum(large, half - 1)
    return base + jnp.where(nn < max_exact, nn, large)


def _bias_tiles(rel_table):
    t = DSA_T
    dlt = jnp.arange(3, dtype=i32)[:, None, None] - 1
    rel = dlt * t + jnp.arange(t, dtype=i32)[None, :, None] - jnp.arange(t, dtype=i32)[None, None, :]
    tiles = jnp.transpose(rel_table.astype(f32)[_t5_bucket(rel)], (0, 3, 1, 2))
    far = jnp.broadcast_to(rel_table.astype(f32)[N_BUCKETS // 2 - 1][:, None], (HC, LANE))
    return tiles, far


def _rope_table(n):
    half = A_ROPE // 2
    inv = ROPE_THETA ** (-jnp.arange(half, dtype=f32) / half)
    ang = jnp.arange(n).astype(f32)[:, None] * inv[None, :]
    cos, sin = jnp.cos(ang), jnp.sin(ang)
    return jnp.concatenate([cos, cos, sin, sin], axis=1)


def _split_hi_lo(w):
    hi = w.astype(bf16)
    lo = (w - hi.astype(f32)).astype(bf16)
    return hi, lo


def kernel(x, meta_tokens, rel_bias_table, ev_w_in, mla_q_norm, mla_kv_norm, mla_w_uq, mla_w_ukv,
           ev_w_out, od_w_in, od_w_out, ln_g, ln_b, router_w, router_b, exp_w_in, exp_b_in,
           exp_w_out, exp_b_out):
    bsz, seq, d = x.shape
    depth = ln_g.shape[0]
    alpha = (2 * depth) ** 0.25
    n_tot = seq + N_META
    n = -(-n_tot // Q_BLOCK) * Q_BLOCK
    m = bsz * n
    assert m % TOK_BLK == 0 and d % (2 * LANE) == 0
    h = jnp.concatenate([jnp.broadcast_to(meta_tokens[None].astype(x.dtype), (bsz, N_META, d)), x,
                         jnp.zeros((bsz, n - n_tot, d), x.dtype)], axis=1).reshape(m, d)
    hb = h.astype(bf16)
    k_sel = min(TOPK_MAX, seq // 4)
    tab = _rope_table(n)
    tiles, far = _bias_tiles(rel_bias_table)
    for i in range(depth):
        j = i // 2
        if i % 2 == 0:
            w_in2, wq2, wkv2 = _prep_even(ev_w_in[j], mla_w_uq[j], mla_w_ukv[j])
            qa, ka, va, sbq, sbk, sbv = _even_proj(hb, w_in2, mla_q_norm[j][None], mla_kv_norm[j][None],
                                                   wq2, wkv2, tab, n)
            out_a = _mla_attention(qa, ka, va, bsz, n)
            out_b = _sb_attention(sbq, sbk, sbv, bsz, n)
            w_o = ev_w_out[j].astype(bf16)
            parts, weights = [out_a, out_b], [w_o[:HA * A_V], w_o[HA * A_V:]]
        else:
            w_main, w_v_t, w_w_t = _prep_odd(od_w_in[j])
            cq, ck, iq, ik, vt, wt = _odd_proj(hb, w_main, w_v_t, w_w_t, n)
            out_c = _dsa_attention(cq, ck, iq, ik, vt, wt, tiles, far, bsz, n, k_sel)
            parts, weights = [out_c], [od_w_out[j].astype(bf16)]
        rw = jnp.pad(router_w[i], ((0, 0), (0, LANE - N_EXPERTS)))
        rwh, rwl = _split_hi_lo(rw)
        rb = jnp.pad(router_b[i], (0, LANE - N_EXPERTS))[None]
        h1, hp, gate, idx = _mix_out(parts, weights, h, ln_g[i, 0][None], ln_b[i, 0][None], rwh, rwl, rb, alpha)
        h, hb = _moe(hp, h1, gate, idx, exp_w_in[i].astype(bf16), exp_b_in[i][:, None, :],
                     exp_w_out[i].astype(bf16), exp_b_out[i][:, None, :],
                     ln_g[i, 1][None], ln_b[i, 1][None], alpha)
    return h.reshape(bsz, n, d)[:, N_META:N_META + seq]
```

```python
import functools
import math

import numpy as np
import jax
import jax.numpy as jnp
from jax import lax
from jax.experimental import pallas as pl
from jax.experimental.pallas import tpu as pltpu

f32 = jnp.float32
bf16 = jnp.bfloat16
i32 = jnp.int32

CHUNK = 64
N_META = 16
Q_BLOCK = 128
HA, A_NOPE, A_ROPE, A_V = 4, 128, 64, 128
Q_LORA, KV_LORA = 512, 256
ROPE_THETA = 10000.0
HB, B_DH = 4, 128
HC, C_DH = 8, 128
IDX_H, IDX_DH = 8, 64
TOPK_MAX = 256
N_BUCKETS, MAX_DIST = 32, 128
N_EXPERTS, TOP_K, D_FF = 32, 4, 512
SWIGLU_LIMIT, SWIGLU_ALPHA = 7.0, 1.702
LN_EPS, RMS_EPS = 1e-5, 1e-6

LANE = 128
SUBLANE = 8
VMEM_LIMIT = 56 * 1024 * 1024

NEG = -1e30
INT_MIN = -(2 ** 31)
CID_SHIFT = CHUNK - N_META
CHUNK_LOG2 = 6
MOE_ROWS = 256
TOK_BLK = 256
SB_EXIT = -110.0


def _cparams(sem):
    return pltpu.CompilerParams(dimension_semantics=sem, vmem_limit_bytes=VMEM_LIMIT)


def _pick(n, cands):
    for c in cands:
        if n % c == 0:
            return c
    raise ValueError(f"no block size for {n}")


def _nt(a, b):
    return lax.dot_general(a, b, (((1,), (1,)), ((), ())), preferred_element_type=f32)


def _rms(x, g):
    return x * lax.rsqrt(jnp.mean(x * x, axis=-1, keepdims=True) + RMS_EPS) * g


EV_N = Q_LORA + KV_LORA + 3 * HB * B_DH + 2 * A_ROPE


def _rope_pair(block, tab, keep_low_only):
    prod = block * tab
    r = prod + pltpu.roll(prod, A_ROPE, 1)
    if keep_low_only:
        lane = lax.broadcasted_iota(i32, r.shape, 1)
        r = jnp.where(lane < A_ROPE, r, 0.0)
    return r


def _even_proj_kernel(x_ref, w_ref, qn_ref, kvn_ref, wuq_ref, wukv_ref, tab_ref,
                      qa_ref, ka_ref, va_ref, sbq_ref, sbk_ref, sbv_ref):
    acc = jnp.dot(x_ref[...], w_ref[...], preferred_element_type=f32)
    o = Q_LORA + KV_LORA
    hb = HB * B_DH
    sbq_ref[...] = (acc[:, o:o + hb] * (B_DH ** -0.5)).astype(bf16)
    sbk_ref[...] = acc[:, o + hb:o + 2 * hb].astype(bf16)
    sbv_ref[...] = acc[:, o + 2 * hb:o + 3 * hb].astype(bf16)
    tab = tab_ref[...]
    k_rope = _rope_pair(acc[:, o + 3 * hb:o + 3 * hb + 2 * A_ROPE], tab, False).astype(bf16)
    qn = _rms(acc[:, :Q_LORA], qn_ref[...]).astype(bf16)
    kvn = _rms(acc[:, Q_LORA:o], kvn_ref[...]).astype(bf16)
    qraw = jnp.dot(qn, wuq_ref[...], preferred_element_type=f32)
    kva = jnp.dot(kvn, wukv_ref[...], preferred_element_type=f32)
    scale = (A_NOPE + A_ROPE) ** -0.5 * LOG2E
    for h in range(HA):
        b0 = h * 2 * LANE
        qa_ref[:, b0:b0 + LANE] = (qraw[:, b0:b0 + LANE] * scale).astype(bf16)
        qr = _rope_pair(qraw[:, b0 + LANE:b0 + 2 * LANE], tab, True)
        qa_ref[:, b0 + LANE:b0 + 2 * LANE] = (qr * scale).astype(bf16)
        ka_ref[:, b0:b0 + LANE] = kva[:, h * A_NOPE:(h + 1) * A_NOPE].astype(bf16)
        ka_ref[:, b0 + LANE:b0 + 2 * LANE] = k_rope
    va_ref[...] = kva[:, HA * A_NOPE:].astype(bf16)


def _even_proj(hb, w_in, q_norm, kv_norm, w_uq, w_ukv, tab, n):
    m, d = hb.shape
    tm = _pick(n, (640, 512, 256, 128))
    nb = n // tm
    const = lambda i: (0, 0)
    row = lambda i: (i, 0)
    outs = [jax.ShapeDtypeStruct((m, c), bf16) for c in
            (HA * 2 * LANE, HA * 2 * LANE, HA * A_V, HB * B_DH, HB * B_DH, HB * B_DH)]
    return pl.pallas_call(
        _even_proj_kernel,
        out_shape=outs,
        grid=(m // tm,),
        in_specs=[pl.BlockSpec((tm, d), row),
                  pl.BlockSpec(w_in.shape, const),
                  pl.BlockSpec(q_norm.shape, const),
                  pl.BlockSpec(kv_norm.shape, const),
                  pl.BlockSpec(w_uq.shape, const),
                  pl.BlockSpec(w_ukv.shape, const),
                  pl.BlockSpec((tm, LANE), lambda i: (i % nb, 0))],
        out_specs=[pl.BlockSpec((tm, s.shape[1]), row) for s in outs],
        compiler_params=_cparams(("parallel",)),
        name="even_proj",
    )(hb, w_in, q_norm, kv_norm, w_uq, w_ukv, tab)


MLA_HG = 2


def _mla_kernel(q_ref, k_ref, v_ref, o_ref, m_ref, l_ref, acc_ref, *, t, nblk):
    qi = pl.program_id(2)
    m_ref[...] = jnp.full(m_ref.shape, NEG, f32)
    l_ref[...] = jnp.zeros(l_ref.shape, f32)
    acc_ref[...] = jnp.zeros(acc_ref.shape, f32)
    qs = [q_ref[:, g * 2 * LANE:(g + 1) * 2 * LANE] for g in range(MLA_HG)]

    def step(off, tk, masked):
        if masked:
            qpos = qi * t + lax.broadcasted_iota(i32, (t, tk), 0)
            kpos = off + lax.broadcasted_iota(i32, (t, tk), 1)
            vis = ((kpos + CID_SHIFT) >> CHUNK_LOG2) <= ((qpos + CID_SHIFT) >> CHUNK_LOG2)
        for g in range(MLA_HG):
            s = _nt(qs[g], k_ref[pl.ds(off, tk), g * 2 * LANE:(g + 1) * 2 * LANE])
            if masked:
                s = jnp.where(vis, s, NEG)
            m_old = m_ref[g]
            m_new = jnp.maximum(m_old, jnp.max(s, axis=1, keepdims=True))
            alpha = jnp.exp2(m_old - m_new)
            p = jnp.exp2(s - m_new)
            l_ref[g] = alpha * l_ref[g] + jnp.sum(p, axis=1, keepdims=True)
            acc_ref[g] = alpha * acc_ref[g] + jnp.dot(p.astype(bf16), v_ref[pl.ds(off, tk), g * A_V:(g + 1) * A_V],
                                                      preferred_element_type=f32)
            m_ref[g] = m_new

    def full_body(j, c):
        step(pl.multiple_of(j * t, t), t, False)
        return c

    lax.fori_loop(0, qi, full_body, 0)
    step(pl.multiple_of(qi * t, t), t, True)

    @pl.when(qi + 1 < nblk)
    def _():
        step(pl.multiple_of((qi + 1) * t, t), Q_BLOCK, True)

    for g in range(MLA_HG):
        o_ref[:, g * A_V:(g + 1) * A_V] = (acc_ref[g] / l_ref[g]).astype(bf16)


def _mla_attention(qa, ka, va, bsz, n):
    m = qa.shape[0]
    t = _pick(n, (640, 512, 256, 128))
    nblk = n // t
    hg = MLA_HG
    return pl.pallas_call(
        functools.partial(_mla_kernel, t=t, nblk=nblk),
        out_shape=jax.ShapeDtypeStruct((m, HA * A_V), bf16),
        grid=(bsz, HA // hg, nblk),
        in_specs=[pl.BlockSpec((t, hg * 2 * LANE), lambda b, h, i: (b * nblk + i, h)),
                  pl.BlockSpec((n, hg * 2 * LANE), lambda b, h, i: (b, h)),
                  pl.BlockSpec((n, hg * A_V), lambda b, h, i: (b, h))],
        out_specs=pl.BlockSpec((t, hg * A_V), lambda b, h, i: (b * nblk + i, h)),
        scratch_shapes=[pltpu.VMEM((hg, t, 1), f32), pltpu.VMEM((hg, t, 1), f32), pltpu.VMEM((hg, t, A_V), f32)],
        compiler_params=_cparams(("parallel", "parallel", "parallel")),
        name="mla_attention",
    )(qa, ka, va)


SB_T = 128


def _sb_kernel(q_ref, k_ref, v_ref, u_ref, o_ref):
    t = SB_T
    qi = pl.program_id(1)
    u = u_ref[...]
    qs = [q_ref[:, g * B_DH:(g + 1) * B_DH] for g in range(HB)]

    def block(j, cs, accs, masked):
        off = pl.multiple_of(j * t, t)
        if masked:
            row = lax.broadcasted_iota(i32, (t, t), 0)
            col = lax.broadcasted_iota(i32, (t, t), 1)
            causal = col < row
        new_c, new_acc = [], []
        for g in range(HB):
            z = _nt(qs[g], k_ref[pl.ds(off, t), g * B_DH:(g + 1) * B_DH])
            lk = -(jnp.maximum(z, 0.0) + jnp.log(1.0 + jnp.exp(-jnp.abs(z))))
            if masked:
                lk = jnp.where(causal, lk, 0.0)
            hi = lk.astype(bf16)
            lo = (lk - hi.astype(f32)).astype(bf16)
            cum = jnp.dot(hi, u, preferred_element_type=f32) + jnp.dot(lo, u, preferred_element_type=f32)
            a = jnp.exp(z + cs[g] + cum)
            if masked:
                a = jnp.where(causal, a, 0.0)
            new_acc.append(accs[g] + jnp.dot(a.astype(bf16), v_ref[pl.ds(off, t), g * B_DH:(g + 1) * B_DH],
                                             preferred_element_type=f32))
            new_c.append(cs[g] + cum[:, 0:1])
        return tuple(new_c), tuple(new_acc)

    cs, accs = block(qi, (jnp.zeros((t, 1), f32),) * HB, (jnp.zeros((t, B_DH), f32),) * HB, True)

    def cond(st):
        j, cs, _ = st
        cmax = cs[0]
        for c in cs[1:]:
            cmax = jnp.maximum(cmax, c)
        return jnp.logical_and(j >= 0, jnp.max(cmax) > SB_EXIT)

    def body(st):
        j, cs, accs = st
        cs, accs = block(j, cs, accs, False)
        return j - 1, cs, accs

    _, _, accs = lax.while_loop(cond, body, (qi - 1, cs, accs))
    for g in range(HB):
        o_ref[:, g * B_DH:(g + 1) * B_DH] = accs[g].astype(bf16)


def _sb_attention(sbq, sbk, sbv, bsz, n):
    m = sbq.shape[0]
    t = SB_T
    nblk = n // t
    u = (np.arange(t)[:, None] >= np.arange(t)[None, :]).astype(np.float32)
    u = jnp.asarray(u, bf16)
    w = HB * B_DH
    return pl.pallas_call(
        _sb_kernel,
        out_shape=jax.ShapeDtypeStruct((m, w), bf16),
        grid=(bsz, nblk),
        in_specs=[pl.BlockSpec((t, w), lambda b, i: (b * nblk + i, 0)),
                  pl.BlockSpec((n, w), lambda b, i: (b, 0)),
                  pl.BlockSpec((n, w), lambda b, i: (b, 0)),
                  pl.BlockSpec((t, t), lambda b, i: (0, 0))],
        out_specs=pl.BlockSpec((t, w), lambda b, i: (b * nblk + i, 0)),
        compiler_params=_cparams(("parallel", "parallel")),
        name="stick_breaking_attention",
    )(sbq, sbk, sbv, u)


def _pack_bf16_pairs(x):
    half = x.shape[1] // 2
    xb = x.astype(bf16).astype(f32)
    lo = lax.bitcast_convert_type(xb[:, :half], i32)
    hi = lax.bitcast_convert_type(xb[:, half:], i32)
    return ((lo >> 16) & 0xFFFF) | (hi & (-65536))


def _unpack_bf16_pairs(w):
    lo = lax.bitcast_convert_type(w << 16, f32).astype(bf16)
    hi = lax.bitcast_convert_type(w & (-65536), f32).astype(bf16)
    return jnp.concatenate([lo, hi], axis=1)


def _layer_norm(y, g, b):
    mu = jnp.mean(y, axis=-1, keepdims=True)
    yc = y - mu
    var = jnp.mean(yc * yc, axis=-1, keepdims=True)
    return yc * lax.rsqrt(var + LN_EPS) * g + b


def _mix_out_kernel(*refs, nparts, alpha):
    a_refs = refs[:nparts]
    w_refs = refs[nparts:2 * nparts]
    h_ref, g_ref, b_ref, rwh_ref, rwl_ref, rb_ref = refs[2 * nparts:2 * nparts + 6]
    h1_ref, hp_ref, gate_ref, idx_ref, cnt_ref = refs[2 * nparts + 6:]
    mix = jnp.dot(a_refs[0][...], w_refs[0][...], preferred_element_type=f32)
    for a_ref, w_ref in zip(a_refs[1:], w_refs[1:]):
        mix = mix + jnp.dot(a_ref[...], w_ref[...], preferred_element_type=f32)
    hn = _layer_norm(alpha * h_ref[...] + mix, g_ref[...], b_ref[...])
    h1_ref[...] = hn
    hp_ref[...] = _pack_bf16_pairs(hn)
    x_hi = hn.astype(bf16)
    x_lo = (hn - x_hi.astype(f32)).astype(bf16)
    rwh = rwh_ref[...]
    logits = (jnp.dot(x_hi, rwh, preferred_element_type=f32) + jnp.dot(x_lo, rwh, preferred_element_type=f32)
              + jnp.dot(x_hi, rwl_ref[...], preferred_element_type=f32) + rb_ref[...])
    lane_i = lax.broadcasted_iota(i32, logits.shape, 1)
    lane = lane_i.astype(f32)
    cur = jnp.where(lane_i < N_EXPERTS, logits, -jnp.inf)
    vals, idxs = [], []
    for _ in range(TOP_K):
        mx = jnp.max(cur, axis=1, keepdims=True)
        ix = jnp.min(jnp.where(cur == mx, lane, float(LANE)), axis=1, keepdims=True)
        vals.append(mx)
        idxs.append(ix)
        cur = jnp.where(lane == ix, -jnp.inf, cur)
    es = [jnp.exp(v - vals[0]) for v in vals]
    den = es[0]
    for e in es[1:]:
        den = den + e
    gate = jnp.zeros(logits.shape, f32)
    idx = jnp.zeros(logits.shape, f32)
    picked = jnp.zeros(logits.shape, f32)
    for k in range(TOP_K):
        gate = jnp.where(lane_i == k, es[k] / den, gate)
        idx = jnp.where(lane_i == k, idxs[k], idx)
        picked = picked + jnp.where(lane == idxs[k], 1.0, 0.0)
    gate_ref[...] = gate
    idx_ref[...] = idx.astype(i32)

    @pl.when(pl.program_id(0) == 0)
    def _():
        cnt_ref[...] = jnp.zeros(cnt_ref.shape, f32)

    cnt_ref[0:1, :] = cnt_ref[0:1, :] + jnp.sum(picked, axis=0, keepdims=True)


def _mix_out(parts, weights, h, g, b, rwh, rwl, rb, alpha):
    m, d = h.shape
    tm = _pick(m, (256, 128))
    row = lambda i: (i, 0)
    const = lambda i: (0, 0)
    nparts = len(parts)
    outs = [jax.ShapeDtypeStruct((m, d), f32), jax.ShapeDtypeStruct((m, d // 2), i32),
            jax.ShapeDtypeStruct((m, LANE), f32), jax.ShapeDtypeStruct((m, LANE), i32),
            jax.ShapeDtypeStruct((SUBLANE, LANE), f32)]
    in_specs = ([pl.BlockSpec((tm, p.shape[1]), row) for p in parts]
                + [pl.BlockSpec(w.shape, const) for w in weights]
                + [pl.BlockSpec((tm, d), row), pl.BlockSpec((1, d), const), pl.BlockSpec((1, d), const),
                   pl.BlockSpec(rwh.shape, const), pl.BlockSpec(rwl.shape, const), pl.BlockSpec((1, LANE), const)])
    return pl.pallas_call(
        functools.partial(_mix_out_kernel, nparts=nparts, alpha=alpha),
        out_shape=outs,
        grid=(m // tm,),
        in_specs=in_specs,
        out_specs=([pl.BlockSpec((tm, s.shape[1]), row) for s in outs[:-1]]
                   + [pl.BlockSpec((SUBLANE, LANE), const)]),
        compiler_params=_cparams(("arbitrary",)),
        name="mix_out_ln_router",
    )(*parts, *weights, h, g, b, rwh, rwl, rb)


def _row_copy(src, dst, sem):
    return pltpu.make_async_copy(src, dst, sem)


ISSUE_UNROLL = 8


def _dest_kernel(idx_ref, tri_ref, start_ref, dest_ref, carry_ref):
    @pl.when(pl.program_id(0) == 0)
    def _():
        carry_ref[...] = jnp.zeros(carry_ref.shape, f32)

    idx = idx_ref[...]
    lane = lax.broadcasted_iota(i32, idx.shape, 1)
    hits = [lane == idx[:, k:k + 1] for k in range(TOP_K)]
    onehot = jnp.where(hits[0], 1.0, 0.0)
    for hit in hits[1:]:
        onehot = onehot + jnp.where(hit, 1.0, 0.0)
    carry = carry_ref[0:1, :]
    base = jnp.dot(tri_ref[...], onehot.astype(bf16), preferred_element_type=f32) + carry + start_ref[...]
    dest = jnp.zeros(idx.shape, f32)
    for k in range(TOP_K):
        dest = jnp.where(lane == k, jnp.sum(jnp.where(hits[k], base, 0.0), axis=1, keepdims=True), dest)
    dest_ref[...] = dest.astype(i32)
    carry_ref[0:1, :] = carry + jnp.sum(onehot, axis=0, keepdims=True)


def _dest_rows(idx, start):
    m = idx.shape[0]
    tm = TOK_BLK
    tri = jnp.asarray(np.tril(np.ones((tm, tm), np.float32), -1), bf16)
    return pl.pallas_call(
        _dest_kernel,
        out_shape=jax.ShapeDtypeStruct((m, LANE), i32),
        grid=(m // tm,),
        in_specs=[pl.BlockSpec((tm, LANE), lambda i: (i, 0)),
                  pl.BlockSpec((tm, tm), lambda i: (0, 0)),
                  pl.BlockSpec((1, LANE), lambda i: (0, 0))],
        out_specs=pl.BlockSpec((tm, LANE), lambda i: (i, 0)),
        scratch_shapes=[pltpu.VMEM((SUBLANE, LANE), f32)],
        compiler_params=_cparams(("arbitrary",)),
        name="moe_dest_rows",
    )(idx, tri, start)


def _dispatch_kernel(lo_ref, hi_ref, dest_hbm, hp_ref, xs_out, dest_smem, zrow, sem_idx, sem_rows, sem_pad):
    i = pl.program_id(0)
    cp = pltpu.make_async_copy(dest_hbm.at[i], dest_smem, sem_idx)
    cp.start()
    cp.wait()
    tm = hp_ref.shape[0]

    def issue(r, c):
        for k in range(TOP_K):
            d = dest_smem[r * TOP_K + k]
            _row_copy(hp_ref.at[pl.ds(r, 1)], xs_out.at[pl.ds(d, 1)], sem_rows).start()
        return c

    lax.fori_loop(0, tm, issue, 0, unroll=ISSUE_UNROLL)
    for k in range(TOP_K):
        _row_copy(hp_ref, xs_out.at[pl.ds(0, tm)], sem_rows).wait()

    @pl.when(i == pl.num_programs(0) - 1)
    def _():
        zrow[...] = jnp.zeros(zrow.shape, i32)

        def per_expert(e, c):
            lo, hi = lo_ref[e], hi_ref[e]

            def zstart(r, c2):
                _row_copy(zrow.at[pl.ds(0, 1)], xs_out.at[pl.ds(r, 1)], sem_pad).start()
                return c2

            def zwait(r, c2):
                _row_copy(zrow.at[pl.ds(0, 1)], xs_out.at[pl.ds(0, 1)], sem_pad).wait()
                return c2

            lax.fori_loop(lo, hi, zstart, 0)
            lax.fori_loop(lo, hi, zwait, 0)
            return c

        lax.fori_loop(0, N_EXPERTS, per_expert, 0)

        rows = zrow.shape[0]
        first = hi_ref[N_EXPERTS - 1] // rows
        last = xs_out.shape[0] // rows

        def bstart(b, c2):
            _row_copy(zrow, xs_out.at[pl.ds(pl.multiple_of(b * rows, rows), rows)], sem_pad).start()
            return c2

        def bwait(b, c2):
            _row_copy(zrow, xs_out.at[pl.ds(0, rows)], sem_pad).wait()
            return c2

        lax.fori_loop(first, last, bstart, 0)
        lax.fori_loop(first, last, bwait, 0)


def _dispatch(pad_lo, pad_hi, dest2d, hp, n_rows):
    m, dh = hp.shape
    tm = TOK_BLK
    grid_spec = pltpu.PrefetchScalarGridSpec(
        num_scalar_prefetch=2,
        grid=(m // tm,),
        in_specs=[pl.BlockSpec(memory_space=pl.ANY),
                  pl.BlockSpec((tm, dh), lambda i, lo, hi: (i, 0))],
        out_specs=pl.BlockSpec(memory_space=pl.ANY),
        scratch_shapes=[pltpu.SMEM((tm * TOP_K,), i32), pltpu.VMEM((MOE_ROWS, dh), i32),
                        pltpu.SemaphoreType.DMA, pltpu.SemaphoreType.DMA, pltpu.SemaphoreType.DMA],
    )
    return pl.pallas_call(
        _dispatch_kernel,
        out_shape=jax.ShapeDtypeStruct((n_rows, dh), i32),
        grid_spec=grid_spec,
        compiler_params=_cparams(("arbitrary",)),
        name="moe_dispatch",
    )(pad_lo, pad_hi, dest2d, hp)


def _expert_kernel(be_ref, nu_ref, x_ref, wi_ref, bi_ref, wo_ref, bo_ref, y_ref, wib_ref, wob_ref):
    i = pl.program_id(0)

    @pl.when(jnp.logical_or(i == 0, be_ref[i] != be_ref[jnp.maximum(i - 1, 0)]))
    def _():
        wib_ref[...] = wi_ref[...].astype(bf16)
        wob_ref[...] = wo_ref[...].astype(bf16)

    @pl.when(i >= nu_ref[0])
    def _():
        y_ref[...] = jnp.zeros(y_ref.shape, f32)

    @pl.when(i < nu_ref[0])
    def _():
        xb = _unpack_bf16_pairs(x_ref[...])
        hh = jnp.dot(xb, wib_ref[...], preferred_element_type=f32) + bi_ref[...]
        gate = jnp.minimum(hh[:, :D_FF], SWIGLU_LIMIT)
        up = jnp.clip(hh[:, D_FF:], -SWIGLU_LIMIT, SWIGLU_LIMIT)
        act = gate * (1.0 / (1.0 + jnp.exp(-SWIGLU_ALPHA * gate))) * (up + 1.0)
        y_ref[...] = jnp.dot(act.astype(bf16), wob_ref[...], preferred_element_type=f32) + bo_ref[...]


def _experts(block_e, n_used, xs, w_in, b_in, w_out, b_out, layer):
    n_rows, dh = xs.shape
    d = 2 * dh
    nb = n_rows // MOE_ROWS
    blk = lambda i, be, nu: (jnp.minimum(i, nu[0] - 1), 0)
    per_e = lambda i, be, nu: (be[i], 0, 0)
    per_le = lambda i, be, nu: (layer, be[i], 0, 0)
    grid_spec = pltpu.PrefetchScalarGridSpec(
        num_scalar_prefetch=2,
        grid=(nb,),
        in_specs=[pl.BlockSpec((MOE_ROWS, dh), blk),
                  pl.BlockSpec((None, None, d, 2 * D_FF), per_le),
                  pl.BlockSpec((None, 1, 2 * D_FF), per_e),
                  pl.BlockSpec((None, None, D_FF, d), per_le),
                  pl.BlockSpec((None, 1, d), per_e)],
        out_specs=pl.BlockSpec((MOE_ROWS, d), lambda i, be, nu: (i, 0)),
        scratch_shapes=[pltpu.VMEM((d, 2 * D_FF), bf16), pltpu.VMEM((D_FF, d), bf16)],
    )
    return pl.pallas_call(
        _expert_kernel,
        out_shape=jax.ShapeDtypeStruct((n_rows, d), f32),
        grid_spec=grid_spec,
        compiler_params=_cparams(("arbitrary",)),
        name="moe_experts",
    )(block_e, n_used, xs, w_in, b_in, w_out, b_out)


def _combine_kernel(dest_hbm, y_hbm, gate_ref, h_ref, g_ref, b_ref, h2_ref, h2b_ref,
                    dest_smem, ybuf, sem_idx, sem_rows, *, alpha):
    i = pl.program_id(0)
    cp = pltpu.make_async_copy(dest_hbm.at[i], dest_smem, sem_idx)
    cp.start()
    cp.wait()
    tm = h_ref.shape[0]

    def issue(r, c):
        for k in range(TOP_K):
            d = dest_smem[r * TOP_K + k]
            _row_copy(y_hbm.at[pl.ds(d, 1)], ybuf.at[k, pl.ds(r, 1)], sem_rows).start()
        return c

    lax.fori_loop(0, tm, issue, 0, unroll=ISSUE_UNROLL)
    for k in range(TOP_K):
        _row_copy(y_hbm.at[pl.ds(0, tm)], ybuf.at[k], sem_rows).wait()
    gate = gate_ref[...]
    f = gate[:, 0:1] * ybuf[0]
    for k in range(1, TOP_K):
        f = f + gate[:, k:k + 1] * ybuf[k]
    hn = _layer_norm(alpha * h_ref[...] + f, g_ref[...], b_ref[...])
    h2_ref[...] = hn
    h2b_ref[...] = hn.astype(bf16)


def _combine(dest2d, y, gate, h, g, b, alpha):
    m, d = h.shape
    tm = TOK_BLK
    row = lambda i: (i, 0)
    const = lambda i: (0, 0)
    return pl.pallas_call(
        functools.partial(_combine_kernel, alpha=alpha),
        out_shape=[jax.ShapeDtypeStruct((m, d), f32), jax.ShapeDtypeStruct((m, d), bf16)],
        grid=(m // tm,),
        in_specs=[pl.BlockSpec(memory_space=pl.ANY),
                  pl.BlockSpec(memory_space=pl.ANY),
                  pl.BlockSpec((tm, LANE), row),
                  pl.BlockSpec((tm, d), row),
                  pl.BlockSpec((1, d), const),
                  pl.BlockSpec((1, d), const)],
        out_specs=[pl.BlockSpec((tm, d), row), pl.BlockSpec((tm, d), row)],
        scratch_shapes=[pltpu.SMEM((tm * TOP_K,), i32), pltpu.VMEM((TOP_K, tm, d), f32),
                        pltpu.SemaphoreType.DMA, pltpu.SemaphoreType.DMA],
        compiler_params=_cparams(("arbitrary",)),
        name="moe_combine_ln",
    )(dest2d, y, gate, h, g, b)


def _route(idx, cnt, m):
    counts = cnt[0, :N_EXPERTS].astype(i32)
    padded = (counts + MOE_ROWS - 1) // MOE_ROWS * MOE_ROWS
    padded_end = jnp.cumsum(padded)
    start_padded = padded_end - padded
    n_blocks = -(-(m * TOP_K) // MOE_ROWS) + N_EXPERTS
    block_e = jnp.minimum(jnp.searchsorted(padded_end, jnp.arange(n_blocks, dtype=i32) * MOE_ROWS, side='right'),
                          N_EXPERTS - 1).astype(i32)
    n_used = (padded_end[-1:] // MOE_ROWS).astype(i32)
    start = jnp.pad(start_padded.astype(f32), (0, LANE - N_EXPERTS))[None]
    dest = _dest_rows(idx, start)
    dest2d = dest[:, :TOP_K].reshape(m // TOK_BLK, TOK_BLK * TOP_K)
    return dest2d, block_e, n_used, start_padded + counts, padded_end, n_blocks * MOE_ROWS


def _moe(hp, h1, gate, idx, cnt, w_in, b_in, w_out, b_out, layer, g, b, alpha):
    m = h1.shape[0]
    dest2d, block_e, n_used, pad_lo, pad_hi, n_rows = _route(idx, cnt, m)
    xs = _dispatch(pad_lo, pad_hi, dest2d, hp, n_rows)
    y = _experts(block_e, n_used, xs, w_in, b_in, w_out, b_out, layer)
    return _combine(dest2d, y, gate, h1, g, b, alpha)


OD_Q = HC * C_DH
VT_ROWS = C_DH + 2 * SUBLANE
LOG2E = math.log2(math.e)


def _odd_proj_kernel(x_ref, w_ref, wv_ref, ww_ref, cq_ref, ck_ref, iq_ref, ik_ref, vt_ref, wt_ref):
    x = x_ref[...]
    acc = jnp.dot(x, w_ref[...], preferred_element_type=f32)
    cq_ref[...] = (acc[:, :OD_Q] * (C_DH ** -0.5 * LOG2E)).astype(bf16)
    ck_ref[...] = acc[:, OD_Q:OD_Q + C_DH].astype(bf16)
    o = OD_Q + C_DH
    iq_ref[...] = (acc[:, o:o + IDX_H * LANE] * (IDX_DH ** -0.5)).astype(bf16)
    ik_ref[...] = acc[:, o + IDX_H * LANE:].astype(bf16)
    vt = _nt(wv_ref[...], x).astype(bf16)
    extra = (VT_ROWS - C_DH, vt.shape[1])
    ones_row = jnp.where(lax.broadcasted_iota(i32, extra, 0) == 0, 1.0, 0.0).astype(bf16)
    vt = jnp.concatenate([vt, ones_row], axis=0)
    for c in range(vt_ref.shape[0]):
        vt_ref[c] = vt[:, c * LANE:(c + 1) * LANE]
    wt_ref[...] = _nt(ww_ref[...], x)[:IDX_H] * (IDX_H ** -0.5)


def _odd_proj(hb, w_main, w_v_t, w_w_t, n):
    m, d = hb.shape
    tm = _pick(n, (640, 512, 256, 128))
    row = lambda i: (i, 0)
    const = lambda i: (0, 0)
    outs = [jax.ShapeDtypeStruct((m, OD_Q), bf16), jax.ShapeDtypeStruct((m, C_DH), bf16),
            jax.ShapeDtypeStruct((m, IDX_H * LANE), bf16), jax.ShapeDtypeStruct((m, LANE), bf16),
            jax.ShapeDtypeStruct((m // LANE, VT_ROWS, LANE), bf16), jax.ShapeDtypeStruct((IDX_H, m), f32)]
    return pl.pallas_call(
        _odd_proj_kernel,
        out_shape=outs,
        grid=(m // tm,),
        in_specs=[pl.BlockSpec((tm, d), row), pl.BlockSpec(w_main.shape, const),
                  pl.BlockSpec(w_v_t.shape, const), pl.BlockSpec(w_w_t.shape, const)],
        out_specs=[pl.BlockSpec((tm, OD_Q), row), pl.BlockSpec((tm, C_DH), row),
                   pl.BlockSpec((tm, IDX_H * LANE), row), pl.BlockSpec((tm, LANE), row),
                   pl.BlockSpec((tm // LANE, VT_ROWS, LANE), lambda i: (i, 0, 0)),
                   pl.BlockSpec((IDX_H, tm), lambda i: (0, i))],
        compiler_params=_cparams(("parallel",)),
        name="odd_proj",
    )(hb, w_main, w_v_t, w_w_t)


DSA_T = 128
IDX_BITS = 14


DSA_SPAN = 4
NO_CUT = 2 ** 30


def _for_blocks(lo, hi, fn):
    def body(i, c):
        fn(lo + 2 * i, 2)
        return c

    lax.fori_loop(0, (hi - lo) >> 1, body, 0)

    @pl.when(((hi - lo) & 1) == 1)
    def _():
        fn(hi - 1, 1)


def _dsa_kernel(cq_ref, iq_ref, wt_ref, ik_ref, ck_ref, vt_ref, tile_ref, o_ref,
                ks_ref, lim_ref, acc_ref, m_ref, *, nblk, k_sel):
    t = DSA_T
    qi = pl.program_id(1)
    nkb = jnp.minimum(qi + 2, nblk)
    kf = float(k_sel)

    def key_pos(off, rows):
        return off + lax.broadcasted_iota(i32, (rows, t), 0)

    iq = iq_ref[...]
    iq_stack = jnp.concatenate([iq[:, h * LANE:(h + 1) * LANE] for h in range(IDX_H)], axis=0)
    wt = wt_ref[...]

    def score_qk(j, nb):
        return _nt(ik_ref[pl.ds(pl.multiple_of(j * t, t), nb * t), :], iq_stack)

    def score_span(j, nb, masked, st=None):
        rows = nb * t
        off = pl.multiple_of(j * t, t)
        if st is None:
            st = score_qk(j, nb)
        sc = jnp.maximum(st[:, :t], 0.0) * wt[0:1, :]
        for h in range(1, IDX_H):
            sc = sc + jnp.maximum(st[:, h * t:(h + 1) * t], 0.0) * wt[h:h + 1, :]
        sc = jnp.where(sc == 0.0, 0.0, sc)
        bits = lax.bitcast_convert_type(sc, i32)
        key = jnp.where(bits < 0, bits ^ 0x7FFFFFFF, bits)
        if masked:
            tpos = qi * t + lax.broadcasted_iota(i32, (rows, t), 1)
            adm = ((key_pos(off, rows) + CID_SHIFT) >> CHUNK_LOG2) <= ((tpos + CID_SHIFT) >> CHUNK_LOG2)
            key = jnp.where(adm, key, INT_MIN)
        ks_ref[pl.ds(off, rows), :] = key

    _for_blocks(0, qi, lambda j, nb: score_span(j, nb, False))
    _for_blocks(qi, nkb, lambda j, nb: score_span(j, nb, True))
    ks_ref[pl.ds(pl.multiple_of(nkb * t, t), (DSA_SPAN - 1) * t), :] = jnp.full(((DSA_SPAN - 1) * t, t), INT_MIN, i32)

    span_rows = DSA_SPAN * t
    nspan = (nkb + DSA_SPAN - 1) // DSA_SPAN

    def count(pred_fn):
        lanes = 8 * SUBLANE

        def body(i, cnt):
            off = pl.multiple_of(i * span_rows, span_rows)
            hit = jnp.where(pred_fn(ks_ref[pl.ds(off, span_rows), :], off), 1.0, 0.0)
            return cnt + jnp.sum(hit.reshape(span_rows // lanes, lanes, t), axis=0)
        cnt = lax.fori_loop(0, nspan, body, jnp.zeros((lanes, t), f32))
        return jnp.sum(cnt, axis=0, keepdims=True)

    n_adm = count(lambda kk, off: kk > INT_MIN)

    def search_cond(st):
        b, _, cnt = st
        return jnp.logical_and(b < 32, jnp.max(cnt) > kf)

    def search_body(st):
        b, thr, cnt = st
        for _ in range(2):
            cand = thr + lax.shift_left(jnp.int32(1), 31 - b)
            c = count(lambda kk, off: kk >= cand)
            ok = c >= kf
            b, thr, cnt = b + 1, jnp.where(ok, cand, thr), jnp.where(ok, c, cnt)
        return b, thr, cnt

    _, thr, cnt = lax.while_loop(search_cond, search_body,
                                 (jnp.int32(0), jnp.full((1, t), INT_MIN, i32), n_adm))
    thr = jnp.maximum(thr, INT_MIN + 1)

    lim_ref[...] = jnp.full(lim_ref.shape, NO_CUT, i32)

    @pl.when(jnp.max(cnt) > kf)
    def _():
        need = kf - count(lambda kk, off: kk > thr)

        def idx_step(b, lim):
            cand = lim + lax.shift_left(jnp.int32(1), IDX_BITS - 1 - b)
            c = count(lambda kk, off: jnp.logical_and(kk == thr, key_pos(off, span_rows) < cand))
            return jnp.where(c <= need, cand, lim)

        lim = lax.fori_loop(0, IDX_BITS, idx_step, jnp.zeros((1, t), i32))
        lim_ref[0:1, :] = jnp.where(cnt > kf, lim, NO_CUT)

    lim = lim_ref[0:1, :]

    cq = cq_ref[...]
    q_stack = jnp.concatenate([cq[:, h * LANE:(h + 1) * LANE] for h in range(HC)], axis=0)
    m_ref[...] = jnp.full(m_ref.shape, NEG, f32)
    acc_ref[...] = jnp.zeros(acc_ref.shape, f32)

    def attn_qk(j, nb):
        return _nt(ck_ref[pl.ds(pl.multiple_of(j * t, t), nb * t), :], q_stack)

    def attn_span(j, nb, near, st=None):
        rows = nb * t
        off = pl.multiple_of(j * t, t)
        kk = ks_ref[pl.ds(off, rows), :]
        sel = kk >= jnp.where(key_pos(off, rows) < lim, thr, thr + 1)
        if st is None:
            st = attn_qk(j, nb)
        m_old = m_ref[...]
        ps, m_news = [], []
        for h in range(HC):
            s = st[:, h * t:(h + 1) * t]
            if near:
                s = s + tile_ref[j - qi + 1, h]
            s = jnp.where(sel, s, NEG)
            m_new = jnp.maximum(m_old[:, h * t:(h + 1) * t], jnp.max(s, axis=0, keepdims=True))
            ps.append(jnp.exp2(s - m_new))
            m_news.append(m_new)
        p = jnp.concatenate(ps, axis=1)
        m_new = jnp.concatenate(m_news, axis=1)
        alpha = jnp.exp2(m_old - m_new)
        vt = vt_ref[j] if nb == 1 else jnp.concatenate([vt_ref[j], vt_ref[j + 1]], axis=1)
        acc_ref[...] = alpha * acc_ref[...] + jnp.dot(vt, p.astype(bf16), preferred_element_type=f32)
        m_ref[...] = m_new

    def near_body(j, c):
        attn_span(j, 1, True)
        return c

    n_far = jnp.maximum(qi - 1, 0)
    _for_blocks(0, n_far, lambda j, nb: attn_span(j, nb, False))
    lax.fori_loop(n_far, nkb, near_body, 0)
    out_t = acc_ref[0:C_DH, :] / acc_ref[C_DH:C_DH + 1, :]
    for h in range(HC):
        o_ref[:, h * C_DH:(h + 1) * C_DH] = out_t[:, h * t:(h + 1) * t].T.astype(bf16)


def _dsa_attention(cq, ck, iq, ik, vt, wt, tiles, bsz, n, k_sel):
    m = cq.shape[0]
    t = DSA_T
    nblk = n // t
    vt4 = vt.reshape(bsz, nblk, VT_ROWS, LANE)
    return pl.pallas_call(
        functools.partial(_dsa_kernel, nblk=nblk, k_sel=k_sel),
        out_shape=jax.ShapeDtypeStruct((m, HC * C_DH), bf16),
        grid=(bsz, nblk),
        in_specs=[pl.BlockSpec((t, HC * C_DH), lambda b, i: (b * nblk + i, 0)),
                  pl.BlockSpec((t, IDX_H * LANE), lambda b, i: (b * nblk + i, 0)),
                  pl.BlockSpec((IDX_H, t), lambda b, i: (0, b * nblk + i)),
                  pl.BlockSpec((n, LANE), lambda b, i: (b, 0)),
                  pl.BlockSpec((n, C_DH), lambda b, i: (b, 0)),
                  pl.BlockSpec((None, nblk, VT_ROWS, LANE), lambda b, i: (b, 0, 0, 0)),
                  pl.BlockSpec(tiles.shape, lambda b, i: (0, 0, 0, 0))],
        out_specs=pl.BlockSpec((t, HC * C_DH), lambda b, i: (b * nblk + i, 0)),
        scratch_shapes=[pltpu.VMEM((n + (DSA_SPAN - 1) * t, t), i32), pltpu.VMEM((SUBLANE, t), i32),
                        pltpu.VMEM((VT_ROWS, HC * t), f32), pltpu.VMEM((1, HC * t), f32)],
        compiler_params=_cparams(("parallel", "parallel")),
        name="dsa_attention",
    )(cq, iq, wt, ik, ck, vt4, tiles)


def _rot_half_cols(w):
    half = w.shape[-1] // 2
    return jnp.concatenate([-w[..., half:], w[..., :half]], axis=-1)


def _prep_even(w_in, w_uq, w_ukv):
    o0 = Q_LORA + KV_LORA
    kr = w_in[:, o0:o0 + A_ROPE]
    w_in2 = jnp.concatenate([w_in[:, :o0], w_in[:, o0 + A_ROPE:], kr, _rot_half_cols(kr)], axis=1).astype(bf16)
    wq = w_uq.reshape(Q_LORA, HA, A_NOPE + A_ROPE)
    wq_r = wq[:, :, A_NOPE:]
    wq2 = jnp.concatenate([wq[:, :, :A_NOPE], wq_r, _rot_half_cols(wq_r)], axis=-1).reshape(Q_LORA, HA * 2 * LANE)
    wkv = w_ukv.reshape(KV_LORA, HA, A_NOPE + A_V)
    wkv2 = jnp.concatenate([wkv[:, :, :A_NOPE].reshape(KV_LORA, HA * A_NOPE),
                            wkv[:, :, A_NOPE:].reshape(KV_LORA, HA * A_V)], axis=1)
    return w_in2, wq2.astype(bf16), wkv2.astype(bf16)


def _prep_odd(w_in):
    d = w_in.shape[0]
    o = OD_Q
    w_q, w_k, w_v = w_in[:, :o], w_in[:, o:o + C_DH], w_in[:, o + C_DH:o + 2 * C_DH]
    o2 = o + 2 * C_DH
    w_iq = w_in[:, o2:o2 + IDX_H * IDX_DH].reshape(d, IDX_H, IDX_DH)
    w_iq = jnp.pad(w_iq, ((0, 0), (0, 0), (0, LANE - IDX_DH))).reshape(d, IDX_H * LANE)
    o3 = o2 + IDX_H * IDX_DH
    w_ik = jnp.pad(w_in[:, o3:o3 + IDX_DH], ((0, 0), (0, LANE - IDX_DH)))
    w_iw = w_in[:, o3 + IDX_DH:o3 + IDX_DH + IDX_H]
    w_main = jnp.concatenate([w_q, w_k, w_iq, w_ik], axis=1).astype(bf16)
    w_v_t = w_v.T.astype(bf16)
    w_w_t = jnp.pad(w_iw.T, ((0, 2 * SUBLANE - IDX_H), (0, 0))).astype(bf16)
    return w_main, w_v_t, w_w_t


def _t5_bucket(rel):
    half = N_BUCKETS // 2
    max_exact = half // 2
    base = jnp.where(rel > 0, half, 0)
    nn = jnp.abs(rel)
    nf = jnp.maximum(nn, 1).astype(f32)
    large = max_exact + (jnp.log(nf / max_exact) / math.log(MAX_DIST / max_exact)
                         * (half - max_exact)).astype(i32)
    large = jnp.minimum(large, half - 1)
    return base + jnp.where(nn < max_exact, nn, large)


def _bias_tiles(rel_table):
    t = DSA_T
    dlt = jnp.arange(3, dtype=i32)[:, None, None] - 1
    rel = dlt * t + jnp.arange(t, dtype=i32)[None, :, None] - jnp.arange(t, dtype=i32)[None, None, :]
    onehot = (_t5_bucket(rel)[..., None] == jnp.arange(N_BUCKETS, dtype=i32)).astype(f32)
    tiles = jnp.einsum('dstb,bh->dhst', onehot, rel_table.astype(f32), precision=lax.Precision.HIGHEST)
    far = rel_table.astype(f32)[N_BUCKETS // 2 - 1]
    return (tiles - far[None, :, None, None]) * LOG2E


def _rope_table(n):
    half = A_ROPE // 2
    inv = ROPE_THETA ** (-jnp.arange(half, dtype=f32) / half)
    ang = jnp.arange(n).astype(f32)[:, None] * inv[None, :]
    cos, sin = jnp.cos(ang), jnp.sin(ang)
    return jnp.concatenate([cos, cos, sin, sin], axis=1)


def _split_hi_lo(w):
    hi = w.astype(bf16)
    lo = (w - hi.astype(f32)).astype(bf16)
    return hi, lo


def kernel(x, meta_tokens, rel_bias_table, ev_w_in, mla_q_norm, mla_kv_norm, mla_w_uq, mla_w_ukv,
           ev_w_out, od_w_in, od_w_out, ln_g, ln_b, router_w, router_b, exp_w_in, exp_b_in,
           exp_w_out, exp_b_out):
    bsz, seq, d = x.shape
    depth = ln_g.shape[0]
    alpha = (2 * depth) ** 0.25
    n_tot = seq + N_META
    n = -(-n_tot // Q_BLOCK) * Q_BLOCK
    m = bsz * n
    assert m % TOK_BLK == 0 and d % (2 * LANE) == 0
    h = jnp.concatenate([jnp.broadcast_to(meta_tokens[None].astype(x.dtype), (bsz, N_META, d)), x,
                         jnp.zeros((bsz, n - n_tot, d), x.dtype)], axis=1).reshape(m, d)
    hb = h.astype(bf16)
    k_sel = min(TOPK_MAX, seq // 4)
    tab = _rope_table(n)
    tiles = _bias_tiles(rel_bias_table)
    for i in range(depth):
        j = i // 2
        if i % 2 == 0:
            w_in2, wq2, wkv2 = _prep_even(ev_w_in[j], mla_w_uq[j], mla_w_ukv[j])
            qa, ka, va, sbq, sbk, sbv = _even_proj(hb, w_in2, mla_q_norm[j][None], mla_kv_norm[j][None],
                                                   wq2, wkv2, tab, n)
            out_a = _mla_attention(qa, ka, va, bsz, n)
            out_b = _sb_attention(sbq, sbk, sbv, bsz, n)
            w_o = ev_w_out[j].astype(bf16)
            parts, weights = [out_a, out_b], [w_o[:HA * A_V], w_o[HA * A_V:]]
        else:
            w_main, w_v_t, w_w_t = _prep_odd(od_w_in[j])
            cq, ck, iq, ik, vt, wt = _odd_proj(hb, w_main, w_v_t, w_w_t, n)
            out_c = _dsa_attention(cq, ck, iq, ik, vt, wt, tiles, bsz, n, k_sel)
            parts, weights = [out_c], [od_w_out[j].astype(bf16)]
        rw = jnp.pad(router_w[i], ((0, 0), (0, LANE - N_EXPERTS)))
        rwh, rwl = _split_hi_lo(rw)
        rb = jnp.pad(router_b[i], (0, LANE - N_EXPERTS))[None]
        h1, hp, gate, idx, cnt = _mix_out(parts, weights, h, ln_g[i, 0][None], ln_b[i, 0][None], rwh, rwl, rb, alpha)
        h, hb = _moe(hp, h1, gate, idx, cnt, exp_w_in, exp_b_in[i][:, None, :],
                     exp_w_out, exp_b_out[i][:, None, :], i,
                     ln_g[i, 1][None], ln_b[i, 1][None], alpha)
    return h.reshape(bsz, n, d)[:, N_META:N_META + seq]
```

```python
import functools
import math

import numpy as np
import jax
import jax.numpy as jnp
from jax import lax
from jax.experimental import pallas as pl
from jax.experimental.pallas import tpu as pltpu

f32 = jnp.float32
bf16 = jnp.bfloat16
i32 = jnp.int32

CHUNK = 64
N_META = 16
Q_BLOCK = 128
HA, A_NOPE, A_ROPE, A_V = 4, 128, 64, 128
Q_LORA, KV_LORA = 512, 256
ROPE_THETA = 10000.0
HB, B_DH = 4, 128
HC, C_DH = 8, 128
IDX_H, IDX_DH = 8, 64
TOPK_MAX = 256
N_BUCKETS, MAX_DIST = 32, 128
N_EXPERTS, TOP_K, D_FF = 32, 4, 512
SWIGLU_LIMIT, SWIGLU_ALPHA = 7.0, 1.702
LN_EPS, RMS_EPS = 1e-5, 1e-6

LANE = 128
SUBLANE = 8
VMEM_LIMIT = 56 * 1024 * 1024

NEG = -1e30
INT_MIN = -(2 ** 31)
CID_SHIFT = CHUNK - N_META
CHUNK_LOG2 = 6
MOE_ROWS = 256
TOK_BLK = 256
SB_EXIT = -110.0


def _cparams(sem):
    return pltpu.CompilerParams(dimension_semantics=sem, vmem_limit_bytes=VMEM_LIMIT)


def _pick(n, cands):
    for c in cands:
        if n % c == 0:
            return c
    raise ValueError(f"no block size for {n}")


def _nt(a, b):
    return lax.dot_general(a, b, (((1,), (1,)), ((), ())), preferred_element_type=f32)


def _rms(x, g):
    return x * lax.rsqrt(jnp.mean(x * x, axis=-1, keepdims=True) + RMS_EPS) * g


EV_N = Q_LORA + KV_LORA + 3 * HB * B_DH + 2 * A_ROPE


def _rope_pair(block, tab, keep_low_only):
    prod = block * tab
    r = prod + pltpu.roll(prod, A_ROPE, 1)
    if keep_low_only:
        lane = lax.broadcasted_iota(i32, r.shape, 1)
        r = jnp.where(lane < A_ROPE, r, 0.0)
    return r


def _even_proj_kernel(x_ref, w_ref, qn_ref, kvn_ref, wuq_ref, wukv_ref, tab_ref,
                      qa_ref, ka_ref, va_ref, sbq_ref, sbk_ref, sbv_ref):
    acc = jnp.dot(x_ref[...], w_ref[...], preferred_element_type=f32)
    o = Q_LORA + KV_LORA
    hb = HB * B_DH
    sbq_ref[...] = (acc[:, o:o + hb] * (B_DH ** -0.5)).astype(bf16)
    sbk_ref[...] = acc[:, o + hb:o + 2 * hb].astype(bf16)
    sbv_ref[...] = acc[:, o + 2 * hb:o + 3 * hb].astype(bf16)
    tab = tab_ref[...]
    k_rope = _rope_pair(acc[:, o + 3 * hb:o + 3 * hb + 2 * A_ROPE], tab, False).astype(bf16)
    qn = _rms(acc[:, :Q_LORA], qn_ref[...]).astype(bf16)
    kvn = _rms(acc[:, Q_LORA:o], kvn_ref[...]).astype(bf16)
    qraw = jnp.dot(qn, wuq_ref[...], preferred_element_type=f32)
    kva = jnp.dot(kvn, wukv_ref[...], preferred_element_type=f32)
    scale = (A_NOPE + A_ROPE) ** -0.5 * LOG2E
    for h in range(HA):
        b0 = h * 2 * LANE
        qa_ref[:, b0:b0 + LANE] = (qraw[:, b0:b0 + LANE] * scale).astype(bf16)
        qr = _rope_pair(qraw[:, b0 + LANE:b0 + 2 * LANE], tab, True)
        qa_ref[:, b0 + LANE:b0 + 2 * LANE] = (qr * scale).astype(bf16)
        ka_ref[:, b0:b0 + LANE] = kva[:, h * A_NOPE:(h + 1) * A_NOPE].astype(bf16)
        ka_ref[:, b0 + LANE:b0 + 2 * LANE] = k_rope
    va_ref[...] = kva[:, HA * A_NOPE:].astype(bf16)


def _even_proj(hb, w_in, q_norm, kv_norm, w_uq, w_ukv, tab, n):
    m, d = hb.shape
    tm = _pick(n, (640, 512, 256, 128))
    nb = n // tm
    const = lambda i: (0, 0)
    row = lambda i: (i, 0)
    outs = [jax.ShapeDtypeStruct((m, c), bf16) for c in
            (HA * 2 * LANE, HA * 2 * LANE, HA * A_V, HB * B_DH, HB * B_DH, HB * B_DH)]
    return pl.pallas_call(
        _even_proj_kernel,
        out_shape=outs,
        grid=(m // tm,),
        in_specs=[pl.BlockSpec((tm, d), row),
                  pl.BlockSpec(w_in.shape, const),
                  pl.BlockSpec(q_norm.shape, const),
                  pl.BlockSpec(kv_norm.shape, const),
                  pl.BlockSpec(w_uq.shape, const),
                  pl.BlockSpec(w_ukv.shape, const),
                  pl.BlockSpec((tm, LANE), lambda i: (i % nb, 0))],
        out_specs=[pl.BlockSpec((tm, s.shape[1]), row) for s in outs],
        compiler_params=_cparams(("parallel",)),
        name="even_proj",
    )(hb, w_in, q_norm, kv_norm, w_uq, w_ukv, tab)


MLA_HG = 2


def _mla_kernel(q_ref, k_ref, v_ref, o_ref, m_ref, l_ref, acc_ref, sa_ref, sb_ref, *, t, nblk):
    qi = pl.program_id(2)
    m_ref[...] = jnp.full(m_ref.shape, NEG, f32)
    l_ref[...] = jnp.zeros(l_ref.shape, f32)
    acc_ref[...] = jnp.zeros(acc_ref.shape, f32)
    qs = [q_ref[:, g * 2 * LANE:(g + 1) * 2 * LANE] for g in range(MLA_HG)]

    def scores(j, dst):
        off = pl.multiple_of(j * t, t)
        for g in range(MLA_HG):
            dst[g] = _nt(qs[g], k_ref[pl.ds(off, t), g * 2 * LANE:(g + 1) * 2 * LANE])

    def step(off, tk, masked, src=None):
        if masked:
            qpos = qi * t + lax.broadcasted_iota(i32, (t, tk), 0)
            kpos = off + lax.broadcasted_iota(i32, (t, tk), 1)
            vis = ((kpos + CID_SHIFT) >> CHUNK_LOG2) <= ((qpos + CID_SHIFT) >> CHUNK_LOG2)
        for g in range(MLA_HG):
            if src is None:
                s = _nt(qs[g], k_ref[pl.ds(off, tk), g * 2 * LANE:(g + 1) * 2 * LANE])
            else:
                s = src[g]
            if masked:
                s = jnp.where(vis, s, NEG)
            m_old = m_ref[g]
            m_new = jnp.maximum(m_old, jnp.max(s, axis=1, keepdims=True))
            alpha = jnp.exp2(m_old - m_new)
            p = jnp.exp2(s - m_new)
            l_ref[g] = alpha * l_ref[g] + jnp.sum(p, axis=1, keepdims=True)
            acc_ref[g] = alpha * acc_ref[g] + jnp.dot(p.astype(bf16), v_ref[pl.ds(off, tk), g * A_V:(g + 1) * A_V],
                                                      preferred_element_type=f32)
            m_ref[g] = m_new

    npair = qi >> 1

    @pl.when(npair > 0)
    def _():
        scores(0, sa_ref)

    def pair_body(i, c):
        j = 2 * i
        scores(j + 1, sb_ref)
        step(pl.multiple_of(j * t, t), t, False, sa_ref)
        scores(2 * jnp.minimum(i + 1, npair - 1), sa_ref)
        step(pl.multiple_of((j + 1) * t, t), t, False, sb_ref)
        return c

    lax.fori_loop(0, npair, pair_body, 0)

    @pl.when((qi & 1) == 1)
    def _():
        step(pl.multiple_of((qi - 1) * t, t), t, False)

    step(pl.multiple_of(qi * t, t), t, True)

    @pl.when(qi + 1 < nblk)
    def _():
        step(pl.multiple_of((qi + 1) * t, t), Q_BLOCK, True)

    for g in range(MLA_HG):
        o_ref[:, g * A_V:(g + 1) * A_V] = (acc_ref[g] / l_ref[g]).astype(bf16)


def _mla_attention(qa, ka, va, bsz, n):
    m = qa.shape[0]
    t = _pick(n, (640, 512, 256, 128))
    nblk = n // t
    hg = MLA_HG
    return pl.pallas_call(
        functools.partial(_mla_kernel, t=t, nblk=nblk),
        out_shape=jax.ShapeDtypeStruct((m, HA * A_V), bf16),
        grid=(bsz, HA // hg, nblk),
        in_specs=[pl.BlockSpec((t, hg * 2 * LANE), lambda b, h, i: (b * nblk + i, h)),
                  pl.BlockSpec((n, hg * 2 * LANE), lambda b, h, i: (b, h)),
                  pl.BlockSpec((n, hg * A_V), lambda b, h, i: (b, h))],
        out_specs=pl.BlockSpec((t, hg * A_V), lambda b, h, i: (b * nblk + i, h)),
        scratch_shapes=[pltpu.VMEM((hg, t, 1), f32), pltpu.VMEM((hg, t, 1), f32), pltpu.VMEM((hg, t, A_V), f32),
                        pltpu.VMEM((hg, t, t), f32), pltpu.VMEM((hg, t, t), f32)],
        compiler_params=_cparams(("parallel", "parallel", "parallel")),
        name="mla_attention",
    )(qa, ka, va)


SB_T = 128


def _sb_kernel(q_ref, k_ref, v_ref, u_ref, o_ref):
    t = SB_T
    qi = pl.program_id(1)
    u = u_ref[...]
    qs = [q_ref[:, g * B_DH:(g + 1) * B_DH] for g in range(HB)]

    def block(j, cs, accs, masked):
        off = pl.multiple_of(j * t, t)
        if masked:
            row = lax.broadcasted_iota(i32, (t, t), 0)
            col = lax.broadcasted_iota(i32, (t, t), 1)
            causal = col < row
        new_c, new_acc = [], []
        for g in range(HB):
            z = _nt(qs[g], k_ref[pl.ds(off, t), g * B_DH:(g + 1) * B_DH])
            lk = -(jnp.maximum(z, 0.0) + jnp.log(1.0 + jnp.exp(-jnp.abs(z))))
            if masked:
                lk = jnp.where(causal, lk, 0.0)
            hi = lk.astype(bf16)
            lo = (lk - hi.astype(f32)).astype(bf16)
            cum = jnp.dot(hi, u, preferred_element_type=f32) + jnp.dot(lo, u, preferred_element_type=f32)
            a = jnp.exp(z + cs[g] + cum)
            if masked:
                a = jnp.where(causal, a, 0.0)
            new_acc.append(accs[g] + jnp.dot(a.astype(bf16), v_ref[pl.ds(off, t), g * B_DH:(g + 1) * B_DH],
                                             preferred_element_type=f32))
            new_c.append(cs[g] + cum[:, 0:1])
        return tuple(new_c), tuple(new_acc)

    cs, accs = block(qi, (jnp.zeros((t, 1), f32),) * HB, (jnp.zeros((t, B_DH), f32),) * HB, True)

    def cond(st):
        j, cs, _ = st
        cmax = cs[0]
        for c in cs[1:]:
            cmax = jnp.maximum(cmax, c)
        return jnp.logical_and(j >= 0, jnp.max(cmax) > SB_EXIT)

    def body(st):
        j, cs, accs = st
        cs, accs = block(j, cs, accs, False)
        return j - 1, cs, accs

    _, _, accs = lax.while_loop(cond, body, (qi - 1, cs, accs))
    for g in range(HB):
        o_ref[:, g * B_DH:(g + 1) * B_DH] = accs[g].astype(bf16)


def _sb_attention(sbq, sbk, sbv, bsz, n):
    m = sbq.shape[0]
    t = SB_T
    nblk = n // t
    u = (np.arange(t)[:, None] >= np.arange(t)[None, :]).astype(np.float32)
    u = jnp.asarray(u, bf16)
    w = HB * B_DH
    return pl.pallas_call(
        _sb_kernel,
        out_shape=jax.ShapeDtypeStruct((m, w), bf16),
        grid=(bsz, nblk),
        in_specs=[pl.BlockSpec((t, w), lambda b, i: (b * nblk + i, 0)),
                  pl.BlockSpec((n, w), lambda b, i: (b, 0)),
                  pl.BlockSpec((n, w), lambda b, i: (b, 0)),
                  pl.BlockSpec((t, t), lambda b, i: (0, 0))],
        out_specs=pl.BlockSpec((t, w), lambda b, i: (b * nblk + i, 0)),
        compiler_params=_cparams(("parallel", "parallel")),
        name="stick_breaking_attention",
    )(sbq, sbk, sbv, u)


def _pack_bf16_pairs(x):
    half = x.shape[1] // 2
    xb = x.astype(bf16).astype(f32)
    lo = lax.bitcast_convert_type(xb[:, :half], i32)
    hi = lax.bitcast_convert_type(xb[:, half:], i32)
    return ((lo >> 16) & 0xFFFF) | (hi & (-65536))


def _unpack_bf16_pairs(w):
    lo = lax.bitcast_convert_type(w << 16, f32).astype(bf16)
    hi = lax.bitcast_convert_type(w & (-65536), f32).astype(bf16)
    return jnp.concatenate([lo, hi], axis=1)


def _layer_norm(y, g, b):
    mu = jnp.mean(y, axis=-1, keepdims=True)
    yc = y - mu
    var = jnp.mean(yc * yc, axis=-1, keepdims=True)
    return yc * lax.rsqrt(var + LN_EPS) * g + b


def _mix_out_kernel(*refs, nparts, alpha):
    a_refs = refs[:nparts]
    w_refs = refs[nparts:2 * nparts]
    h_ref, g_ref, b_ref, rwh_ref, rwl_ref, rb_ref = refs[2 * nparts:2 * nparts + 6]
    h1_ref, hp_ref, gate_ref, idx_ref, cnt_ref = refs[2 * nparts + 6:]
    mix = jnp.dot(a_refs[0][...], w_refs[0][...], preferred_element_type=f32)
    for a_ref, w_ref in zip(a_refs[1:], w_refs[1:]):
        mix = mix + jnp.dot(a_ref[...], w_ref[...], preferred_element_type=f32)
    hn = _layer_norm(alpha * h_ref[...] + mix, g_ref[...], b_ref[...])
    h1_ref[...] = hn
    hp_ref[...] = _pack_bf16_pairs(hn)
    x_hi = hn.astype(bf16)
    x_lo = (hn - x_hi.astype(f32)).astype(bf16)
    rwh = rwh_ref[...]
    logits = (jnp.dot(x_hi, rwh, preferred_element_type=f32) + jnp.dot(x_lo, rwh, preferred_element_type=f32)
              + jnp.dot(x_hi, rwl_ref[...], preferred_element_type=f32) + rb_ref[...])
    lane_i = lax.broadcasted_iota(i32, logits.shape, 1)
    lane = lane_i.astype(f32)
    cur = jnp.where(lane_i < N_EXPERTS, logits, -jnp.inf)
    vals, idxs = [], []
    for _ in range(TOP_K):
        mx = jnp.max(cur, axis=1, keepdims=True)
        ix = jnp.min(jnp.where(cur == mx, lane, float(LANE)), axis=1, keepdims=True)
        vals.append(mx)
        idxs.append(ix)
        cur = jnp.where(lane == ix, -jnp.inf, cur)
    es = [jnp.exp(v - vals[0]) for v in vals]
    den = es[0]
    for e in es[1:]:
        den = den + e
    gate = jnp.zeros(logits.shape, f32)
    idx = jnp.zeros(logits.shape, f32)
    picked = jnp.zeros(logits.shape, f32)
    for k in range(TOP_K):
        gate = jnp.where(lane_i == k, es[k] / den, gate)
        idx = jnp.where(lane_i == k, idxs[k], idx)
        picked = picked + jnp.where(lane == idxs[k], 1.0, 0.0)
    gate_ref[...] = gate
    idx_ref[...] = idx.astype(i32)

    @pl.when(pl.program_id(0) == 0)
    def _():
        cnt_ref[...] = jnp.zeros(cnt_ref.shape, f32)

    cnt_ref[0:1, :] = cnt_ref[0:1, :] + jnp.sum(picked, axis=0, keepdims=True)


def _mix_out(parts, weights, h, g, b, rwh, rwl, rb, alpha):
    m, d = h.shape
    tm = _pick(m, (256, 128))
    row = lambda i: (i, 0)
    const = lambda i: (0, 0)
    nparts = len(parts)
    outs = [jax.ShapeDtypeStruct((m, d), f32), jax.ShapeDtypeStruct((m, d // 2), i32),
            jax.ShapeDtypeStruct((m, LANE), f32), jax.ShapeDtypeStruct((m, LANE), i32),
            jax.ShapeDtypeStruct((SUBLANE, LANE), f32)]
    in_specs = ([pl.BlockSpec((tm, p.shape[1]), row) for p in parts]
                + [pl.BlockSpec(w.shape, const) for w in weights]
                + [pl.BlockSpec((tm, d), row), pl.BlockSpec((1, d), const), pl.BlockSpec((1, d), const),
                   pl.BlockSpec(rwh.shape, const), pl.BlockSpec(rwl.shape, const), pl.BlockSpec((1, LANE), const)])
    return pl.pallas_call(
        functools.partial(_mix_out_kernel, nparts=nparts, alpha=alpha),
        out_shape=outs,
        grid=(m // tm,),
        in_specs=in_specs,
        out_specs=([pl.BlockSpec((tm, s.shape[1]), row) for s in outs[:-1]]
                   + [pl.BlockSpec((SUBLANE, LANE), const)]),
        compiler_params=_cparams(("arbitrary",)),
        name="mix_out_ln_router",
    )(*parts, *weights, h, g, b, rwh, rwl, rb)


def _row_copy(src, dst, sem):
    return pltpu.make_async_copy(src, dst, sem)


def _dest_kernel(idx_ref, tri_ref, start_ref, dest_ref, carry_ref):
    @pl.when(pl.program_id(0) == 0)
    def _():
        carry_ref[...] = jnp.zeros(carry_ref.shape, f32)

    idx = idx_ref[...]
    lane = lax.broadcasted_iota(i32, idx.shape, 1)
    hits = [lane == idx[:, k:k + 1] for k in range(TOP_K)]
    onehot = jnp.where(hits[0], 1.0, 0.0)
    for hit in hits[1:]:
        onehot = onehot + jnp.where(hit, 1.0, 0.0)
    carry = carry_ref[0:1, :]
    base = jnp.dot(tri_ref[...], onehot.astype(bf16), preferred_element_type=f32) + carry + start_ref[...]
    dest = jnp.zeros(idx.shape, f32)
    for k in range(TOP_K):
        dest = jnp.where(lane == k, jnp.sum(jnp.where(hits[k], base, 0.0), axis=1, keepdims=True), dest)
    dest_ref[...] = dest.T[0:SUBLANE, :].astype(i32)
    carry_ref[0:1, :] = carry + jnp.sum(onehot, axis=0, keepdims=True)


def _dest_rows(idx, start):
    m = idx.shape[0]
    tm = TOK_BLK
    tri = jnp.asarray(np.tril(np.ones((tm, tm), np.float32), -1), bf16)
    return pl.pallas_call(
        _dest_kernel,
        out_shape=jax.ShapeDtypeStruct((m // tm, SUBLANE, tm), i32),
        grid=(m // tm,),
        in_specs=[pl.BlockSpec((tm, LANE), lambda i: (i, 0)),
                  pl.BlockSpec((tm, tm), lambda i: (0, 0)),
                  pl.BlockSpec((1, LANE), lambda i: (0, 0))],
        out_specs=pl.BlockSpec((None, SUBLANE, tm), lambda i: (i, 0, 0)),
        scratch_shapes=[pltpu.VMEM((SUBLANE, LANE), f32)],
        compiler_params=_cparams(("arbitrary",)),
        name="moe_dest_rows",
    )(idx, tri, start)


def _dispatch_kernel(lo_ref, hi_ref, dest_hbm, hp_ref, xs_out, dest_smem, zrow, sem_idx, sem_rows, sem_pad):
    i = pl.program_id(0)
    cp = pltpu.make_async_copy(dest_hbm.at[i], dest_smem, sem_idx)
    cp.start()
    cp.wait()
    tm = hp_ref.shape[0]

    def issue(g, c):
        base = pl.multiple_of(g * SUBLANE, SUBLANE)
        for s in range(SUBLANE):
            for k in range(TOP_K):
                d = dest_smem[k * tm + base + s]
                _row_copy(hp_ref.at[pl.ds(base + s, 1)], xs_out.at[pl.ds(d, 1)], sem_rows).start()
        return c

    lax.fori_loop(0, tm // SUBLANE, issue, 0)
    for k in range(TOP_K):
        _row_copy(hp_ref, xs_out.at[pl.ds(0, tm)], sem_rows).wait()

    @pl.when(i == pl.num_programs(0) - 1)
    def _():
        zrow[...] = jnp.zeros(zrow.shape, i32)

        def per_expert(e, c):
            lo, hi = lo_ref[e], hi_ref[e]

            def zstart(r, c2):
                _row_copy(zrow.at[pl.ds(0, 1)], xs_out.at[pl.ds(r, 1)], sem_pad).start()
                return c2

            def zwait(r, c2):
                _row_copy(zrow.at[pl.ds(0, 1)], xs_out.at[pl.ds(0, 1)], sem_pad).wait()
                return c2

            lax.fori_loop(lo, hi, zstart, 0)
            lax.fori_loop(lo, hi, zwait, 0)
            return c

        lax.fori_loop(0, N_EXPERTS, per_expert, 0)

        rows = zrow.shape[0]
        first = hi_ref[N_EXPERTS - 1] // rows
        last = xs_out.shape[0] // rows

        def bstart(b, c2):
            _row_copy(zrow, xs_out.at[pl.ds(pl.multiple_of(b * rows, rows), rows)], sem_pad).start()
            return c2

        def bwait(b, c2):
            _row_copy(zrow, xs_out.at[pl.ds(0, rows)], sem_pad).wait()
            return c2

        lax.fori_loop(first, last, bstart, 0)
        lax.fori_loop(first, last, bwait, 0)


def _dispatch(pad_lo, pad_hi, dest2d, hp, n_rows):
    m, dh = hp.shape
    tm = TOK_BLK
    grid_spec = pltpu.PrefetchScalarGridSpec(
        num_scalar_prefetch=2,
        grid=(m // tm,),
        in_specs=[pl.BlockSpec(memory_space=pl.ANY),
                  pl.BlockSpec((tm, dh), lambda i, lo, hi: (i, 0))],
        out_specs=pl.BlockSpec(memory_space=pl.ANY),
        scratch_shapes=[pltpu.SMEM((tm * TOP_K,), i32), pltpu.VMEM((MOE_ROWS, dh), i32),
                        pltpu.SemaphoreType.DMA, pltpu.SemaphoreType.DMA, pltpu.SemaphoreType.DMA],
    )
    return pl.pallas_call(
        _dispatch_kernel,
        out_shape=jax.ShapeDtypeStruct((n_rows, dh), i32),
        grid_spec=grid_spec,
        compiler_params=_cparams(("arbitrary",)),
        name="moe_dispatch",
    )(pad_lo, pad_hi, dest2d, hp)


def _expert_kernel(be_ref, nu_ref, x_ref, wi_ref, bi_ref, wo_ref, bo_ref, y_ref, wib_ref, wob_ref):
    i = pl.program_id(0)

    @pl.when(jnp.logical_or(i == 0, be_ref[i] != be_ref[jnp.maximum(i - 1, 0)]))
    def _():
        wib_ref[...] = wi_ref[...].astype(bf16)
        wob_ref[...] = wo_ref[...].astype(bf16)

    @pl.when(i >= nu_ref[0])
    def _():
        y_ref[...] = jnp.zeros(y_ref.shape, f32)

    @pl.when(i < nu_ref[0])
    def _():
        xb = _unpack_bf16_pairs(x_ref[...])
        hh = jnp.dot(xb, wib_ref[...], preferred_element_type=f32) + bi_ref[...]
        gate = jnp.minimum(hh[:, :D_FF], SWIGLU_LIMIT)
        up = jnp.clip(hh[:, D_FF:], -SWIGLU_LIMIT, SWIGLU_LIMIT)
        act = gate * (1.0 / (1.0 + jnp.exp(-SWIGLU_ALPHA * gate))) * (up + 1.0)
        y_ref[...] = jnp.dot(act.astype(bf16), wob_ref[...], preferred_element_type=f32) + bo_ref[...]


def _experts(block_e, n_used, xs, w_in, b_in, w_out, b_out, layer):
    n_rows, dh = xs.shape
    d = 2 * dh
    nb = n_rows // MOE_ROWS
    blk = lambda i, be, nu: (jnp.minimum(i, nu[0] - 1), 0)
    per_e = lambda i, be, nu: (be[i], 0, 0)
    per_le = lambda i, be, nu: (layer, be[i], 0, 0)
    grid_spec = pltpu.PrefetchScalarGridSpec(
        num_scalar_prefetch=2,
        grid=(nb,),
        in_specs=[pl.BlockSpec((MOE_ROWS, dh), blk),
                  pl.BlockSpec((None, None, d, 2 * D_FF), per_le),
                  pl.BlockSpec((None, 1, 2 * D_FF), per_e),
                  pl.BlockSpec((None, None, D_FF, d), per_le),
                  pl.BlockSpec((None, 1, d), per_e)],
        out_specs=pl.BlockSpec((MOE_ROWS, d), lambda i, be, nu: (i, 0)),
        scratch_shapes=[pltpu.VMEM((d, 2 * D_FF), bf16), pltpu.VMEM((D_FF, d), bf16)],
    )
    return pl.pallas_call(
        _expert_kernel,
        out_shape=jax.ShapeDtypeStruct((n_rows, d), f32),
        grid_spec=grid_spec,
        compiler_params=_cparams(("arbitrary",)),
        name="moe_experts",
    )(block_e, n_used, xs, w_in, b_in, w_out, b_out)


def _combine_kernel(dest_hbm, y_hbm, gate_ref, h_ref, g_ref, b_ref, h2_ref, h2b_ref,
                    dest_smem, ybuf, sem_idx, sem_rows, *, alpha):
    i = pl.program_id(0)
    cp = pltpu.make_async_copy(dest_hbm.at[i], dest_smem, sem_idx)
    cp.start()
    cp.wait()
    tm = h_ref.shape[0]

    def issue(g, c):
        base = pl.multiple_of(g * SUBLANE, SUBLANE)
        for s in range(SUBLANE):
            for k in range(TOP_K):
                d = dest_smem[k * tm + base + s]
                _row_copy(y_hbm.at[pl.ds(d, 1)], ybuf.at[k, pl.ds(base + s, 1)], sem_rows).start()
        return c

    lax.fori_loop(0, tm // SUBLANE, issue, 0)
    for k in range(TOP_K):
        _row_copy(y_hbm.at[pl.ds(0, tm)], ybuf.at[k], sem_rows).wait()
    gate = gate_ref[...]
    f = gate[:, 0:1] * ybuf[0]
    for k in range(1, TOP_K):
        f = f + gate[:, k:k + 1] * ybuf[k]
    hn = _layer_norm(alpha * h_ref[...] + f, g_ref[...], b_ref[...])
    h2_ref[...] = hn
    h2b_ref[...] = hn.astype(bf16)


def _combine(dest2d, y, gate, h, g, b, alpha):
    m, d = h.shape
    tm = TOK_BLK
    row = lambda i: (i, 0)
    const = lambda i: (0, 0)
    return pl.pallas_call(
        functools.partial(_combine_kernel, alpha=alpha),
        out_shape=[jax.ShapeDtypeStruct((m, d), f32), jax.ShapeDtypeStruct((m, d), bf16)],
        grid=(m // tm,),
        in_specs=[pl.BlockSpec(memory_space=pl.ANY),
                  pl.BlockSpec(memory_space=pl.ANY),
                  pl.BlockSpec((tm, LANE), row),
                  pl.BlockSpec((tm, d), row),
                  pl.BlockSpec((1, d), const),
                  pl.BlockSpec((1, d), const)],
        out_specs=[pl.BlockSpec((tm, d), row), pl.BlockSpec((tm, d), row)],
        scratch_shapes=[pltpu.SMEM((tm * TOP_K,), i32), pltpu.VMEM((TOP_K, tm, d), f32),
                        pltpu.SemaphoreType.DMA, pltpu.SemaphoreType.DMA],
        compiler_params=_cparams(("arbitrary",)),
        name="moe_combine_ln",
    )(dest2d, y, gate, h, g, b)


def _route(idx, cnt, m):
    counts = cnt[0, :N_EXPERTS].astype(i32)
    padded = (counts + MOE_ROWS - 1) // MOE_ROWS * MOE_ROWS
    padded_end = jnp.cumsum(padded)
    start_padded = padded_end - padded
    n_blocks = -(-(m * TOP_K) // MOE_ROWS) + N_EXPERTS
    first_row = jnp.arange(n_blocks, dtype=i32) * MOE_ROWS
    block_e = jnp.minimum(jnp.sum((first_row[:, None] >= padded_end[None, :]).astype(i32), axis=1), N_EXPERTS - 1)
    n_used = (padded_end[-1:] // MOE_ROWS).astype(i32)
    start = jnp.pad(start_padded.astype(f32), (0, LANE - N_EXPERTS))[None]
    dest = _dest_rows(idx, start)
    dest2d = dest[:, :TOP_K, :].reshape(m // TOK_BLK, TOP_K * TOK_BLK)
    return dest2d, block_e, n_used, start_padded + counts, padded_end, n_blocks * MOE_ROWS


def _moe(hp, h1, gate, idx, cnt, w_in, b_in, w_out, b_out, layer, g, b, alpha):
    m = h1.shape[0]
    dest2d, block_e, n_used, pad_lo, pad_hi, n_rows = _route(idx, cnt, m)
    xs = _dispatch(pad_lo, pad_hi, dest2d, hp, n_rows)
    y = _experts(block_e, n_used, xs, w_in, b_in, w_out, b_out, layer)
    return _combine(dest2d, y, gate, h1, g, b, alpha)


OD_Q = HC * C_DH
VT_ROWS = C_DH + 2 * SUBLANE
LOG2E = math.log2(math.e)


def _odd_proj_kernel(x_ref, w_ref, wv_ref, ww_ref, cq_ref, ck_ref, iq_ref, ik_ref, vt_ref, wt_ref):
    x = x_ref[...]
    acc = jnp.dot(x, w_ref[...], preferred_element_type=f32)
    cq_ref[...] = (acc[:, :OD_Q] * (C_DH ** -0.5 * LOG2E)).astype(bf16)
    ck_ref[...] = acc[:, OD_Q:OD_Q + C_DH].astype(bf16)
    o = OD_Q + C_DH
    iq_ref[...] = (acc[:, o:o + IDX_H * LANE] * (IDX_DH ** -0.5)).astype(bf16)
    ik_ref[...] = acc[:, o + IDX_H * LANE:].astype(bf16)
    vt = _nt(wv_ref[...], x).astype(bf16)
    extra = (VT_ROWS - C_DH, vt.shape[1])
    ones_row = jnp.where(lax.broadcasted_iota(i32, extra, 0) == 0, 1.0, 0.0).astype(bf16)
    vt = jnp.concatenate([vt, ones_row], axis=0)
    for c in range(vt_ref.shape[0]):
        vt_ref[c] = vt[:, c * LANE:(c + 1) * LANE]
    wt_ref[...] = _nt(ww_ref[...], x)[:IDX_H] * (IDX_H ** -0.5)


def _odd_proj(hb, w_main, w_v_t, w_w_t, n):
    m, d = hb.shape
    tm = _pick(n, (640, 512, 256, 128))
    row = lambda i: (i, 0)
    const = lambda i: (0, 0)
    outs = [jax.ShapeDtypeStruct((m, OD_Q), bf16), jax.ShapeDtypeStruct((m, C_DH), bf16),
            jax.ShapeDtypeStruct((m, IDX_H * LANE), bf16), jax.ShapeDtypeStruct((m, LANE), bf16),
            jax.ShapeDtypeStruct((m // LANE, VT_ROWS, LANE), bf16), jax.ShapeDtypeStruct((IDX_H, m), f32)]
    return pl.pallas_call(
        _odd_proj_kernel,
        out_shape=outs,
        grid=(m // tm,),
        in_specs=[pl.BlockSpec((tm, d), row), pl.BlockSpec(w_main.shape, const),
                  pl.BlockSpec(w_v_t.shape, const), pl.BlockSpec(w_w_t.shape, const)],
        out_specs=[pl.BlockSpec((tm, OD_Q), row), pl.BlockSpec((tm, C_DH), row),
                   pl.BlockSpec((tm, IDX_H * LANE), row), pl.BlockSpec((tm, LANE), row),
                   pl.BlockSpec((tm // LANE, VT_ROWS, LANE), lambda i: (i, 0, 0)),
                   pl.BlockSpec((IDX_H, tm), lambda i: (0, i))],
        compiler_params=_cparams(("parallel",)),
        name="odd_proj",
    )(hb, w_main, w_v_t, w_w_t)


DSA_T = 128
IDX_BITS = 14


DSA_SPAN = 4
NO_CUT = 2 ** 30


def _for_blocks(lo, hi, fn):
    def body(i, c):
        fn(lo + 2 * i, 2)
        return c

    lax.fori_loop(0, (hi - lo) >> 1, body, 0)

    @pl.when(((hi - lo) & 1) == 1)
    def _():
        fn(hi - 1, 1)


def _pipelined_blocks(hi, qk, use, sta, stb):
    nquad = hi >> 2

    @pl.when(nquad > 0)
    def _():
        sta[...] = qk(0, 2)

    def body(i, c):
        j = 4 * i
        stb[...] = qk(j + 2, 2)
        use(j, 2, sta)
        sta[...] = qk(4 * jnp.minimum(i + 1, nquad - 1), 2)
        use(j + 2, 2, stb)
        return c

    lax.fori_loop(0, nquad, body, 0)
    _for_blocks(4 * nquad, hi, lambda j, nb: use(j, nb, None))


def _dsa_kernel(cq_ref, iq_ref, wt_ref, ik_ref, ck_ref, vt_ref, tile_ref, o_ref,
                ks_ref, lim_ref, acc_ref, m_ref, sta_ref, stb_ref, *, nblk, k_sel):
    t = DSA_T
    qi = pl.program_id(1)
    nkb = jnp.minimum(qi + 2, nblk)
    kf = float(k_sel)

    def key_pos(off, rows):
        return off + lax.broadcasted_iota(i32, (rows, t), 0)

    iq = iq_ref[...]
    iq_stack = jnp.concatenate([iq[:, h * LANE:(h + 1) * LANE] for h in range(IDX_H)], axis=0)
    wt = wt_ref[...]

    def score_qk(j, nb):
        return _nt(ik_ref[pl.ds(pl.multiple_of(j * t, t), nb * t), :], iq_stack)

    def score_span(j, nb, masked, st=None):
        rows = nb * t
        off = pl.multiple_of(j * t, t)
        if st is None:
            st = score_qk(j, nb)
        sc = jnp.maximum(st[:, :t], 0.0) * wt[0:1, :]
        for h in range(1, IDX_H):
            sc = sc + jnp.maximum(st[:, h * t:(h + 1) * t], 0.0) * wt[h:h + 1, :]
        sc = jnp.where(sc == 0.0, 0.0, sc)
        bits = lax.bitcast_convert_type(sc, i32)
        key = jnp.where(bits < 0, bits ^ 0x7FFFFFFF, bits)
        if masked:
            tpos = qi * t + lax.broadcasted_iota(i32, (rows, t), 1)
            adm = ((key_pos(off, rows) + CID_SHIFT) >> CHUNK_LOG2) <= ((tpos + CID_SHIFT) >> CHUNK_LOG2)
            key = jnp.where(adm, key, INT_MIN)
        ks_ref[pl.ds(off, rows), :] = key

    _pipelined_blocks(qi, score_qk, lambda j, nb, st: score_span(j, nb, False, st), sta_ref, stb_ref)
    _for_blocks(qi, nkb, lambda j, nb: score_span(j, nb, True))
    ks_ref[pl.ds(pl.multiple_of(nkb * t, t), (DSA_SPAN - 1) * t), :] = jnp.full(((DSA_SPAN - 1) * t, t), INT_MIN, i32)

    span_rows = DSA_SPAN * t
    nspan = (nkb + DSA_SPAN - 1) // DSA_SPAN

    def count(pred_fn):
        lanes = 8 * SUBLANE

        def body(i, cnt):
            off = pl.multiple_of(i * span_rows, span_rows)
            hit = jnp.where(pred_fn(ks_ref[pl.ds(off, span_rows), :], off), 1.0, 0.0)
            return cnt + jnp.sum(hit.reshape(span_rows // lanes, lanes, t), axis=0)
        cnt = lax.fori_loop(0, nspan, body, jnp.zeros((lanes, t), f32))
        return jnp.sum(cnt, axis=0, keepdims=True)

    n_adm = count(lambda kk, off: kk > INT_MIN)

    def search_cond(st):
        b, _, cnt = st
        return jnp.logical_and(b < 32, jnp.max(cnt) > kf)

    def search_body(st):
        b, thr, cnt = st
        for _ in range(2):
            cand = thr + lax.shift_left(jnp.int32(1), 31 - b)
            c = count(lambda kk, off: kk >= cand)
            ok = c >= kf
            b, thr, cnt = b + 1, jnp.where(ok, cand, thr), jnp.where(ok, c, cnt)
        return b, thr, cnt

    _, thr, cnt = lax.while_loop(search_cond, search_body,
                                 (jnp.int32(0), jnp.full((1, t), INT_MIN, i32), n_adm))
    thr = jnp.maximum(thr, INT_MIN + 1)

    lim_ref[...] = jnp.full(lim_ref.shape, NO_CUT, i32)

    @pl.when(jnp.max(cnt) > kf)
    def _():
        need = kf - count(lambda kk, off: kk > thr)

        def idx_step(b, lim):
            cand = lim + lax.shift_left(jnp.int32(1), IDX_BITS - 1 - b)
            c = count(lambda kk, off: jnp.logical_and(kk == thr, key_pos(off, span_rows) < cand))
            return jnp.where(c <= need, cand, lim)

        lim = lax.fori_loop(0, IDX_BITS, idx_step, jnp.zeros((1, t), i32))
        lim_ref[0:1, :] = jnp.where(cnt > kf, lim, NO_CUT)

    lim = lim_ref[0:1, :]

    cq = cq_ref[...]
    q_stack = jnp.concatenate([cq[:, h * LANE:(h + 1) * LANE] for h in range(HC)], axis=0)
    m_ref[...] = jnp.full(m_ref.shape, NEG, f32)
    acc_ref[...] = jnp.zeros(acc_ref.shape, f32)

    def attn_qk(j, nb):
        return _nt(ck_ref[pl.ds(pl.multiple_of(j * t, t), nb * t), :], q_stack)

    def attn_span(j, nb, near, st=None):
        rows = nb * t
        off = pl.multiple_of(j * t, t)
        kk = ks_ref[pl.ds(off, rows), :]
        sel = kk >= jnp.where(key_pos(off, rows) < lim, thr, thr + 1)
        if st is None:
            st = attn_qk(j, nb)
        m_old = m_ref[...]
        ps, m_news = [], []
        for h in range(HC):
            s = st[:, h * t:(h + 1) * t]
            if near:
                s = s + tile_ref[j - qi + 1, h]
            s = jnp.where(sel, s, NEG)
            m_new = jnp.maximum(m_old[:, h * t:(h + 1) * t], jnp.max(s, axis=0, keepdims=True))
            ps.append(jnp.exp2(s - m_new))
            m_news.append(m_new)
        p = jnp.concatenate(ps, axis=1)
        m_new = jnp.concatenate(m_news, axis=1)
        alpha = jnp.exp2(m_old - m_new)
        vt = vt_ref[j] if nb == 1 else jnp.concatenate([vt_ref[j], vt_ref[j + 1]], axis=1)
        acc_ref[...] = alpha * acc_ref[...] + jnp.dot(vt, p.astype(bf16), preferred_element_type=f32)
        m_ref[...] = m_new

    def near_body(j, c):
        attn_span(j, 1, True)
        return c

    n_far = jnp.maximum(qi - 1, 0)
    _pipelined_blocks(n_far, attn_qk, lambda j, nb, st: attn_span(j, nb, False, st), sta_ref, stb_ref)
    lax.fori_loop(n_far, nkb, near_body, 0)
    out_t = acc_ref[0:C_DH, :] / acc_ref[C_DH:C_DH + 1, :]
    for h in range(HC):
        o_ref[:, h * C_DH:(h + 1) * C_DH] = out_t[:, h * t:(h + 1) * t].T.astype(bf16)


def _dsa_attention(cq, ck, iq, ik, vt, wt, tiles, bsz, n, k_sel):
    m = cq.shape[0]
    t = DSA_T
    nblk = n // t
    vt4 = vt.reshape(bsz, nblk, VT_ROWS, LANE)
    return pl.pallas_call(
        functools.partial(_dsa_kernel, nblk=nblk, k_sel=k_sel),
        out_shape=jax.ShapeDtypeStruct((m, HC * C_DH), bf16),
        grid=(bsz, nblk),
        in_specs=[pl.BlockSpec((t, HC * C_DH), lambda b, i: (b * nblk + i, 0)),
                  pl.BlockSpec((t, IDX_H * LANE), lambda b, i: (b * nblk + i, 0)),
                  pl.BlockSpec((IDX_H, t), lambda b, i: (0, b * nblk + i)),
                  pl.BlockSpec((n, LANE), lambda b, i: (b, 0)),
                  pl.BlockSpec((n, C_DH), lambda b, i: (b, 0)),
                  pl.BlockSpec((None, nblk, VT_ROWS, LANE), lambda b, i: (b, 0, 0, 0)),
                  pl.BlockSpec(tiles.shape, lambda b, i: (0, 0, 0, 0))],
        out_specs=pl.BlockSpec((t, HC * C_DH), lambda b, i: (b * nblk + i, 0)),
        scratch_shapes=[pltpu.VMEM((n + (DSA_SPAN - 1) * t, t), i32), pltpu.VMEM((SUBLANE, t), i32),
                        pltpu.VMEM((VT_ROWS, HC * t), f32), pltpu.VMEM((1, HC * t), f32),
                        pltpu.VMEM((2 * t, HC * t), f32), pltpu.VMEM((2 * t, HC * t), f32)],
        compiler_params=_cparams(("parallel", "parallel")),
        name="dsa_attention",
    )(cq, iq, wt, ik, ck, vt4, tiles)


def _rot_half_cols(w):
    half = w.shape[-1] // 2
    return jnp.concatenate([-w[..., half:], w[..., :half]], axis=-1)


def _prep_even(w_in, w_uq, w_ukv):
    o0 = Q_LORA + KV_LORA
    kr = w_in[:, o0:o0 + A_ROPE]
    w_in2 = jnp.concatenate([w_in[:, :o0], w_in[:, o0 + A_ROPE:], kr, _rot_half_cols(kr)], axis=1).astype(bf16)
    wq = w_uq.reshape(Q_LORA, HA, A_NOPE + A_ROPE)
    wq_r = wq[:, :, A_NOPE:]
    wq2 = jnp.concatenate([wq[:, :, :A_NOPE], wq_r, _rot_half_cols(wq_r)], axis=-1).reshape(Q_LORA, HA * 2 * LANE)
    wkv = w_ukv.reshape(KV_LORA, HA, A_NOPE + A_V)
    wkv2 = jnp.concatenate([wkv[:, :, :A_NOPE].reshape(KV_LORA, HA * A_NOPE),
                            wkv[:, :, A_NOPE:].reshape(KV_LORA, HA * A_V)], axis=1)
    return w_in2, wq2.astype(bf16), wkv2.astype(bf16)


def _prep_odd(w_in):
    d = w_in.shape[0]
    o = OD_Q
    w_q, w_k, w_v = w_in[:, :o], w_in[:, o:o + C_DH], w_in[:, o + C_DH:o + 2 * C_DH]
    o2 = o + 2 * C_DH
    w_iq = w_in[:, o2:o2 + IDX_H * IDX_DH].reshape(d, IDX_H, IDX_DH)
    w_iq = jnp.pad(w_iq, ((0, 0), (0, 0), (0, LANE - IDX_DH))).reshape(d, IDX_H * LANE)
    o3 = o2 + IDX_H * IDX_DH
    w_ik = jnp.pad(w_in[:, o3:o3 + IDX_DH], ((0, 0), (0, LANE - IDX_DH)))
    w_iw = w_in[:, o3 + IDX_DH:o3 + IDX_DH + IDX_H]
    w_main = jnp.concatenate([w_q, w_k, w_iq, w_ik], axis=1).astype(bf16)
    w_v_t = w_v.T.astype(bf16)
    w_w_t = jnp.pad(w_iw.T, ((0, 2 * SUBLANE - IDX_H), (0, 0))).astype(bf16)
    return w_main, w_v_t, w_w_t


def _t5_bucket(rel):
    half = N_BUCKETS // 2
    max_exact = half // 2
    base = jnp.where(rel > 0, half, 0)
    nn = jnp.abs(rel)
    nf = jnp.maximum(nn, 1).astype(f32)
    large = max_exact + (jnp.log(nf / max_exact) / math.log(MAX_DIST / max_exact)
                         * (half - max_exact)).astype(i32)
    large = jnp.minimum(large, half - 1)
    return base + jnp.where(nn < max_exact, nn, large)


def _bias_tiles(rel_table):
    t = DSA_T
    dlt = jnp.arange(3, dtype=i32)[:, None, None] - 1
    rel = dlt * t + jnp.arange(t, dtype=i32)[None, :, None] - jnp.arange(t, dtype=i32)[None, None, :]
    onehot = (_t5_bucket(rel)[..., None] == jnp.arange(N_BUCKETS, dtype=i32)).astype(f32)
    tiles = jnp.einsum('dstb,bh->dhst', onehot, rel_table.astype(f32), precision=lax.Precision.HIGHEST)
    far = rel_table.astype(f32)[N_BUCKETS // 2 - 1]
    return (tiles - far[None, :, None, None]) * LOG2E


def _rope_table(n):
    half = A_ROPE // 2
    inv = ROPE_THETA ** (-jnp.arange(half, dtype=f32) / half)
    ang = jnp.arange(n).astype(f32)[:, None] * inv[None, :]
    cos, sin = jnp.cos(ang), jnp.sin(ang)
    return jnp.concatenate([cos, cos, sin, sin], axis=1)


def _split_hi_lo(w):
    hi = w.astype(bf16)
    lo = (w - hi.astype(f32)).astype(bf16)
    return hi, lo


def kernel(x, meta_tokens, rel_bias_table, ev_w_in, mla_q_norm, mla_kv_norm, mla_w_uq, mla_w_ukv,
           ev_w_out, od_w_in, od_w_out, ln_g, ln_b, router_w, router_b, exp_w_in, exp_b_in,
           exp_w_out, exp_b_out):
    bsz, seq, d = x.shape
    depth = ln_g.shape[0]
    alpha = (2 * depth) ** 0.25
    n_tot = seq + N_META
    n = -(-n_tot // Q_BLOCK) * Q_BLOCK
    m = bsz * n
    assert m % TOK_BLK == 0 and d % (2 * LANE) == 0
    h = jnp.concatenate([jnp.broadcast_to(meta_tokens[None].astype(x.dtype), (bsz, N_META, d)), x,
                         jnp.zeros((bsz, n - n_tot, d), x.dtype)], axis=1).reshape(m, d)
    hb = h.astype(bf16)
    k_sel = min(TOPK_MAX, seq // 4)
    tab = _rope_table(n)
    tiles = _bias_tiles(rel_bias_table)
    for i in range(depth):
        j = i // 2
        if i % 2 == 0:
            w_in2, wq2, wkv2 = _prep_even(ev_w_in[j], mla_w_uq[j], mla_w_ukv[j])
            qa, ka, va, sbq, sbk, sbv = _even_proj(hb, w_in2, mla_q_norm[j][None], mla_kv_norm[j][None],
                                                   wq2, wkv2, tab, n)
            out_a = _mla_attention(qa, ka, va, bsz, n)
            out_b = _sb_attention(sbq, sbk, sbv, bsz, n)
            w_o = ev_w_out[j].astype(bf16)
            parts, weights = [out_a, out_b], [w_o[:HA * A_V], w_o[HA * A_V:]]
        else:
            w_main, w_v_t, w_w_t = _prep_odd(od_w_in[j])
            cq, ck, iq, ik, vt, wt = _odd_proj(hb, w_main, w_v_t, w_w_t, n)
            out_c = _dsa_attention(cq, ck, iq, ik, vt, wt, tiles, bsz, n, k_sel)
            parts, weights = [out_c], [od_w_out[j].astype(bf16)]
        rw = jnp.pad(router_w[i], ((0, 0), (0, LANE - N_EXPERTS)))
        rwh, rwl = _split_hi_lo(rw)
        rb = jnp.pad(router_b[i], (0, LANE - N_EXPERTS))[None]
        h1, hp, gate, idx, cnt = _mix_out(parts, weights, h, ln_g[i, 0][None], ln_b[i, 0][None], rwh, rwl, rb, alpha)
        h, hb = _moe(hp, h1, gate, idx, cnt, exp_w_in, exp_b_in[i][:, None, :],
                     exp_w_out, exp_b_out[i][:, None, :], i,
                     ln_g[i, 1][None], ln_b[i, 1][None], alpha)
    return h.reshape(bsz, n, d)[:, N_META:N_META + seq]
```

```python
import functools
import math

import numpy as np
import jax
import jax.numpy as jnp
from jax import lax
from jax.experimental import pallas as pl
from jax.experimental.pallas import tpu as pltpu

f32 = jnp.float32
bf16 = jnp.bfloat16
i32 = jnp.int32
i16 = jnp.int16

CHUNK = 64
N_META = 16
Q_BLOCK = 128
HA, A_NOPE, A_ROPE, A_V = 4, 128, 64, 128
Q_LORA, KV_LORA = 512, 256
ROPE_THETA = 10000.0
HB, B_DH = 4, 128
HC, C_DH = 8, 128
IDX_H, IDX_DH = 8, 64
TOPK_MAX = 256
N_BUCKETS, MAX_DIST = 32, 128
N_EXPERTS, TOP_K, D_FF = 32, 4, 512
SWIGLU_LIMIT, SWIGLU_ALPHA = 7.0, 1.702
LN_EPS, RMS_EPS = 1e-5, 1e-6

LANE = 128
SUBLANE = 8
VMEM_LIMIT = 56 * 1024 * 1024

NEG = -1e30
INT_MIN = -(2 ** 31)
CID_SHIFT = CHUNK - N_META
CHUNK_LOG2 = 6
MOE_ROWS = 256
TOK_BLK = 256
SB_EXIT = -110.0


def _cparams(sem):
    return pltpu.CompilerParams(dimension_semantics=sem, vmem_limit_bytes=VMEM_LIMIT)


def _pick(n, cands):
    for c in cands:
        if n % c == 0:
            return c
    raise ValueError(f"no block size for {n}")


def _nt(a, b):
    return lax.dot_general(a, b, (((1,), (1,)), ((), ())), preferred_element_type=f32)


def _rms(x, g):
    return x * lax.rsqrt(jnp.mean(x * x, axis=-1, keepdims=True) + RMS_EPS) * g


EV_N = Q_LORA + KV_LORA + 3 * HB * B_DH + 2 * A_ROPE


def _rope_pair(block, tab, keep_low_only):
    prod = block * tab
    r = prod + pltpu.roll(prod, A_ROPE, 1)
    if keep_low_only:
        lane = lax.broadcasted_iota(i32, r.shape, 1)
        r = jnp.where(lane < A_ROPE, r, 0.0)
    return r


def _even_proj_kernel(x_ref, w_ref, qn_ref, kvn_ref, wuq_ref, wukv_ref, tab_ref,
                      qa_ref, ka_ref, va_ref, sbq_ref, sbk_ref, sbv_ref):
    acc = jnp.dot(x_ref[...], w_ref[...], preferred_element_type=f32)
    o = Q_LORA + KV_LORA
    hb = HB * B_DH
    sbq_ref[...] = (acc[:, o:o + hb] * (B_DH ** -0.5)).astype(bf16)
    sbk_ref[...] = acc[:, o + hb:o + 2 * hb].astype(bf16)
    sbv_ref[...] = acc[:, o + 2 * hb:o + 3 * hb].astype(bf16)
    tab = tab_ref[...]
    k_rope = _rope_pair(acc[:, o + 3 * hb:o + 3 * hb + 2 * A_ROPE], tab, False).astype(bf16)
    qn = _rms(acc[:, :Q_LORA], qn_ref[...]).astype(bf16)
    kvn = _rms(acc[:, Q_LORA:o], kvn_ref[...]).astype(bf16)
    qraw = jnp.dot(qn, wuq_ref[...], preferred_element_type=f32)
    kva = jnp.dot(kvn, wukv_ref[...], preferred_element_type=f32)
    scale = (A_NOPE + A_ROPE) ** -0.5 * LOG2E
    for h in range(HA):
        b0 = h * 2 * LANE
        qa_ref[:, b0:b0 + LANE] = (qraw[:, b0:b0 + LANE] * scale).astype(bf16)
        qr = _rope_pair(qraw[:, b0 + LANE:b0 + 2 * LANE], tab, True)
        qa_ref[:, b0 + LANE:b0 + 2 * LANE] = (qr * scale).astype(bf16)
        ka_ref[:, b0:b0 + LANE] = kva[:, h * A_NOPE:(h + 1) * A_NOPE].astype(bf16)
        ka_ref[:, b0 + LANE:b0 + 2 * LANE] = k_rope
    va_ref[...] = kva[:, HA * A_NOPE:].astype(bf16)


def _even_proj(hb, w_in, q_norm, kv_norm, w_uq, w_ukv, tab, n):
    m, d = hb.shape
    tm = _pick(n, (640, 512, 256, 128))
    nb = n // tm
    const = lambda i: (0, 0)
    row = lambda i: (i, 0)
    outs = [jax.ShapeDtypeStruct((m, c), bf16) for c in
            (HA * 2 * LANE, HA * 2 * LANE, HA * A_V, HB * B_DH, HB * B_DH, HB * B_DH)]
    return pl.pallas_call(
        _even_proj_kernel,
        out_shape=outs,
        grid=(m // tm,),
        in_specs=[pl.BlockSpec((tm, d), row),
                  pl.BlockSpec(w_in.shape, const),
                  pl.BlockSpec(q_norm.shape, const),
                  pl.BlockSpec(kv_norm.shape, const),
                  pl.BlockSpec(w_uq.shape, const),
                  pl.BlockSpec(w_ukv.shape, const),
                  pl.BlockSpec((tm, LANE), lambda i: (i % nb, 0))],
        out_specs=[pl.BlockSpec((tm, s.shape[1]), row) for s in outs],
        compiler_params=_cparams(("parallel",)),
        name="even_proj",
    )(hb, w_in, q_norm, kv_norm, w_uq, w_ukv, tab)


MLA_HG = 2


def _mla_kernel(q_ref, k_ref, v_ref, o_ref, m_ref, l_ref, acc_ref, sa_ref, sb_ref, *, t, nblk):
    qi = pl.program_id(2)
    m_ref[...] = jnp.full(m_ref.shape, NEG, f32)
    l_ref[...] = jnp.zeros(l_ref.shape, f32)
    acc_ref[...] = jnp.zeros(acc_ref.shape, f32)
    qs = [q_ref[:, g * 2 * LANE:(g + 1) * 2 * LANE] for g in range(MLA_HG)]

    def scores(j, dst):
        off = pl.multiple_of(j * t, t)
        for g in range(MLA_HG):
            dst[g] = _nt(qs[g], k_ref[pl.ds(off, t), g * 2 * LANE:(g + 1) * 2 * LANE])

    def step(off, tk, masked, src=None):
        if masked:
            qpos = qi * t + lax.broadcasted_iota(i32, (t, tk), 0)
            kpos = off + lax.broadcasted_iota(i32, (t, tk), 1)
            vis = ((kpos + CID_SHIFT) >> CHUNK_LOG2) <= ((qpos + CID_SHIFT) >> CHUNK_LOG2)
        for g in range(MLA_HG):
            if src is None:
                s = _nt(qs[g], k_ref[pl.ds(off, tk), g * 2 * LANE:(g + 1) * 2 * LANE])
            else:
                s = src[g]
            if masked:
                s = jnp.where(vis, s, NEG)
            m_old = m_ref[g]
            m_new = jnp.maximum(m_old, jnp.max(s, axis=1, keepdims=True))
            alpha = jnp.exp2(m_old - m_new)
            p = jnp.exp2(s - m_new)
            l_ref[g] = alpha * l_ref[g] + jnp.sum(p, axis=1, keepdims=True)
            acc_ref[g] = alpha * acc_ref[g] + jnp.dot(p.astype(bf16), v_ref[pl.ds(off, tk), g * A_V:(g + 1) * A_V],
                                                      preferred_element_type=f32)
            m_ref[g] = m_new

    npair = qi >> 1

    @pl.when(npair > 0)
    def _():
        scores(0, sa_ref)

    def pair_body(i, c):
        j = 2 * i
        scores(j + 1, sb_ref)
        step(pl.multiple_of(j * t, t), t, False, sa_ref)
        scores(2 * jnp.minimum(i + 1, npair - 1), sa_ref)
        step(pl.multiple_of((j + 1) * t, t), t, False, sb_ref)
        return c

    lax.fori_loop(0, npair, pair_body, 0)

    @pl.when((qi & 1) == 1)
    def _():
        step(pl.multiple_of((qi - 1) * t, t), t, False)

    step(pl.multiple_of(qi * t, t), t, True)

    @pl.when(qi + 1 < nblk)
    def _():
        step(pl.multiple_of((qi + 1) * t, t), Q_BLOCK, True)

    for g in range(MLA_HG):
        o_ref[:, g * A_V:(g + 1) * A_V] = (acc_ref[g] / l_ref[g]).astype(bf16)


def _mla_attention(qa, ka, va, bsz, n):
    m = qa.shape[0]
    t = _pick(n, (640, 512, 256, 128))
    nblk = n // t
    hg = MLA_HG
    return pl.pallas_call(
        functools.partial(_mla_kernel, t=t, nblk=nblk),
        out_shape=jax.ShapeDtypeStruct((m, HA * A_V), bf16),
        grid=(bsz, HA // hg, nblk),
        in_specs=[pl.BlockSpec((t, hg * 2 * LANE), lambda b, h, i: (b * nblk + i, h)),
                  pl.BlockSpec((n, hg * 2 * LANE), lambda b, h, i: (b, h)),
                  pl.BlockSpec((n, hg * A_V), lambda b, h, i: (b, h))],
        out_specs=pl.BlockSpec((t, hg * A_V), lambda b, h, i: (b * nblk + i, h)),
        scratch_shapes=[pltpu.VMEM((hg, t, 1), f32), pltpu.VMEM((hg, t, 1), f32), pltpu.VMEM((hg, t, A_V), f32),
                        pltpu.VMEM((hg, t, t), f32), pltpu.VMEM((hg, t, t), f32)],
        compiler_params=_cparams(("parallel", "parallel", "parallel")),
        name="mla_attention",
    )(qa, ka, va)


SB_T = 128


def _sb_kernel(q_ref, k_ref, v_ref, u_ref, o_ref):
    t = SB_T
    qi = pl.program_id(1)
    u = u_ref[...]
    qs = [q_ref[:, g * B_DH:(g + 1) * B_DH] for g in range(HB)]

    def block(j, cs, accs, masked):
        off = pl.multiple_of(j * t, t)
        if masked:
            row = lax.broadcasted_iota(i32, (t, t), 0)
            col = lax.broadcasted_iota(i32, (t, t), 1)
            causal = col < row
        new_c, new_acc = [], []
        for g in range(HB):
            z = _nt(qs[g], k_ref[pl.ds(off, t), g * B_DH:(g + 1) * B_DH])
            lk = -(jnp.maximum(z, 0.0) + jnp.log(1.0 + jnp.exp(-jnp.abs(z))))
            if masked:
                lk = jnp.where(causal, lk, 0.0)
            hi = lk.astype(bf16)
            lo = (lk - hi.astype(f32)).astype(bf16)
            cum = jnp.dot(hi, u, preferred_element_type=f32) + jnp.dot(lo, u, preferred_element_type=f32)
            a = jnp.exp(z + cs[g] + cum)
            if masked:
                a = jnp.where(causal, a, 0.0)
            new_acc.append(accs[g] + jnp.dot(a.astype(bf16), v_ref[pl.ds(off, t), g * B_DH:(g + 1) * B_DH],
                                             preferred_element_type=f32))
            new_c.append(cs[g] + cum[:, 0:1])
        return tuple(new_c), tuple(new_acc)

    cs, accs = block(qi, (jnp.zeros((t, 1), f32),) * HB, (jnp.zeros((t, B_DH), f32),) * HB, True)

    def cond(st):
        j, cs, _ = st
        cmax = cs[0]
        for c in cs[1:]:
            cmax = jnp.maximum(cmax, c)
        return jnp.logical_and(j >= 0, jnp.max(cmax) > SB_EXIT)

    def body(st):
        j, cs, accs = st
        cs, accs = block(j, cs, accs, False)
        return j - 1, cs, accs

    _, _, accs = lax.while_loop(cond, body, (qi - 1, cs, accs))
    for g in range(HB):
        o_ref[:, g * B_DH:(g + 1) * B_DH] = accs[g].astype(bf16)


def _sb_attention(sbq, sbk, sbv, bsz, n):
    m = sbq.shape[0]
    t = SB_T
    nblk = n // t
    u = (np.arange(t)[:, None] >= np.arange(t)[None, :]).astype(np.float32)
    u = jnp.asarray(u, bf16)
    w = HB * B_DH
    return pl.pallas_call(
        _sb_kernel,
        out_shape=jax.ShapeDtypeStruct((m, w), bf16),
        grid=(bsz, nblk),
        in_specs=[pl.BlockSpec((t, w), lambda b, i: (b * nblk + i, 0)),
                  pl.BlockSpec((n, w), lambda b, i: (b, 0)),
                  pl.BlockSpec((n, w), lambda b, i: (b, 0)),
                  pl.BlockSpec((t, t), lambda b, i: (0, 0))],
        out_specs=pl.BlockSpec((t, w), lambda b, i: (b * nblk + i, 0)),
        compiler_params=_cparams(("parallel", "parallel")),
        name="stick_breaking_attention",
    )(sbq, sbk, sbv, u)


def _pack_bf16_pairs(x):
    half = x.shape[1] // 2
    xb = x.astype(bf16).astype(f32)
    lo = lax.bitcast_convert_type(xb[:, :half], i32)
    hi = lax.bitcast_convert_type(xb[:, half:], i32)
    return ((lo >> 16) & 0xFFFF) | (hi & (-65536))


def _unpack_bf16_pairs(w):
    lo = lax.bitcast_convert_type(w << 16, f32).astype(bf16)
    hi = lax.bitcast_convert_type(w & (-65536), f32).astype(bf16)
    return jnp.concatenate([lo, hi], axis=1)


def _layer_norm(y, g, b):
    mu = jnp.mean(y, axis=-1, keepdims=True)
    yc = y - mu
    var = jnp.mean(yc * yc, axis=-1, keepdims=True)
    return yc * lax.rsqrt(var + LN_EPS) * g + b


def _mix_out_kernel(*refs, nparts, alpha):
    a_refs = refs[:nparts]
    w_refs = refs[nparts:2 * nparts]
    h_ref, g_ref, b_ref, rwh_ref, rwl_ref, rb_ref = refs[2 * nparts:2 * nparts + 6]
    h1_ref, hp_ref, gate_ref, idx_ref, cnt_ref = refs[2 * nparts + 6:]
    mix = jnp.dot(a_refs[0][...], w_refs[0][...], preferred_element_type=f32)
    for a_ref, w_ref in zip(a_refs[1:], w_refs[1:]):
        mix = mix + jnp.dot(a_ref[...], w_ref[...], preferred_element_type=f32)
    hn = _layer_norm(alpha * h_ref[...] + mix, g_ref[...], b_ref[...])
    h1_ref[...] = hn
    hp_ref[...] = _pack_bf16_pairs(hn)
    x_hi = hn.astype(bf16)
    x_lo = (hn - x_hi.astype(f32)).astype(bf16)
    rwh = rwh_ref[...]
    logits = (jnp.dot(x_hi, rwh, preferred_element_type=f32) + jnp.dot(x_lo, rwh, preferred_element_type=f32)
              + jnp.dot(x_hi, rwl_ref[...], preferred_element_type=f32) + rb_ref[...])
    lane_i = lax.broadcasted_iota(i32, logits.shape, 1)
    lane = lane_i.astype(f32)
    cur = jnp.where(lane_i < N_EXPERTS, logits, -jnp.inf)
    vals, idxs = [], []
    for _ in range(TOP_K):
        mx = jnp.max(cur, axis=1, keepdims=True)
        ix = jnp.min(jnp.where(cur == mx, lane, float(LANE)), axis=1, keepdims=True)
        vals.append(mx)
        idxs.append(ix)
        cur = jnp.where(lane == ix, -jnp.inf, cur)
    es = [jnp.exp(v - vals[0]) for v in vals]
    den = es[0]
    for e in es[1:]:
        den = den + e
    gate = jnp.zeros(logits.shape, f32)
    idx = jnp.zeros(logits.shape, f32)
    picked = jnp.zeros(logits.shape, f32)
    for k in range(TOP_K):
        gate = jnp.where(lane_i == k, es[k] / den, gate)
        idx = jnp.where(lane_i == k, idxs[k], idx)
        picked = picked + jnp.where(lane == idxs[k], 1.0, 0.0)
    gate_ref[...] = gate
    idx_ref[...] = idx.astype(i32)

    @pl.when(pl.program_id(0) == 0)
    def _():
        cnt_ref[...] = jnp.zeros(cnt_ref.shape, f32)

    cnt_ref[0:1, :] = cnt_ref[0:1, :] + jnp.sum(picked, axis=0, keepdims=True)


def _mix_out(parts, weights, h, g, b, rwh, rwl, rb, alpha):
    m, d = h.shape
    tm = _pick(m, (256, 128))
    row = lambda i: (i, 0)
    const = lambda i: (0, 0)
    nparts = len(parts)
    outs = [jax.ShapeDtypeStruct((m, d), f32), jax.ShapeDtypeStruct((m, d // 2), i32),
            jax.ShapeDtypeStruct((m, LANE), f32), jax.ShapeDtypeStruct((m, LANE), i32),
            jax.ShapeDtypeStruct((SUBLANE, LANE), f32)]
    in_specs = ([pl.BlockSpec((tm, p.shape[1]), row) for p in parts]
                + [pl.BlockSpec(w.shape, const) for w in weights]
                + [pl.BlockSpec((tm, d), row), pl.BlockSpec((1, d), const), pl.BlockSpec((1, d), const),
                   pl.BlockSpec(rwh.shape, const), pl.BlockSpec(rwl.shape, const), pl.BlockSpec((1, LANE), const)])
    return pl.pallas_call(
        functools.partial(_mix_out_kernel, nparts=nparts, alpha=alpha),
        out_shape=outs,
        grid=(m // tm,),
        in_specs=in_specs,
        out_specs=([pl.BlockSpec((tm, s.shape[1]), row) for s in outs[:-1]]
                   + [pl.BlockSpec((SUBLANE, LANE), const)]),
        compiler_params=_cparams(("arbitrary",)),
        name="mix_out_ln_router",
    )(*parts, *weights, h, g, b, rwh, rwl, rb)


def _row_copy(src, dst, sem):
    return pltpu.make_async_copy(src, dst, sem)


def _dest_kernel(idx_ref, tri_ref, start_ref, dest_ref, carry_ref):
    @pl.when(pl.program_id(0) == 0)
    def _():
        carry_ref[...] = jnp.zeros(carry_ref.shape, f32)

    idx = idx_ref[...]
    lane = lax.broadcasted_iota(i32, idx.shape, 1)
    hits = [lane == idx[:, k:k + 1] for k in range(TOP_K)]
    onehot = jnp.where(hits[0], 1.0, 0.0)
    for hit in hits[1:]:
        onehot = onehot + jnp.where(hit, 1.0, 0.0)
    carry = carry_ref[0:1, :]
    base = jnp.dot(tri_ref[...], onehot.astype(bf16), preferred_element_type=f32) + carry + start_ref[...]
    dest = jnp.zeros(idx.shape, f32)
    for k in range(TOP_K):
        dest = jnp.where(lane == k, jnp.sum(jnp.where(hits[k], base, 0.0), axis=1, keepdims=True), dest)
    dest_ref[...] = dest.T[0:SUBLANE, :].astype(i32)
    carry_ref[0:1, :] = carry + jnp.sum(onehot, axis=0, keepdims=True)


def _dest_rows(idx, start):
    m = idx.shape[0]
    tm = TOK_BLK
    tri = jnp.asarray(np.tril(np.ones((tm, tm), np.float32), -1), bf16)
    return pl.pallas_call(
        _dest_kernel,
        out_shape=jax.ShapeDtypeStruct((m // tm, SUBLANE, tm), i32),
        grid=(m // tm,),
        in_specs=[pl.BlockSpec((tm, LANE), lambda i: (i, 0)),
                  pl.BlockSpec((tm, tm), lambda i: (0, 0)),
                  pl.BlockSpec((1, LANE), lambda i: (0, 0))],
        out_specs=pl.BlockSpec((None, SUBLANE, tm), lambda i: (i, 0, 0)),
        scratch_shapes=[pltpu.VMEM((SUBLANE, LANE), f32)],
        compiler_params=_cparams(("arbitrary",)),
        name="moe_dest_rows",
    )(idx, tri, start)


def _dispatch_kernel(lo_ref, hi_ref, dest_hbm, hp_ref, xs_out, dest_smem, zrow, sem_idx, sem_rows, sem_pad):
    i = pl.program_id(0)
    cp = pltpu.make_async_copy(dest_hbm.at[i], dest_smem, sem_idx)
    cp.start()
    cp.wait()
    tm = hp_ref.shape[0]

    def issue(g, c):
        base = pl.multiple_of(g * SUBLANE, SUBLANE)
        for s in range(SUBLANE):
            for k in range(TOP_K):
                d = dest_smem[k * tm + base + s]
                _row_copy(hp_ref.at[pl.ds(base + s, 1)], xs_out.at[pl.ds(d, 1)], sem_rows).start()
        return c

    lax.fori_loop(0, tm // SUBLANE, issue, 0)
    for k in range(TOP_K):
        _row_copy(hp_ref, xs_out.at[pl.ds(0, tm)], sem_rows).wait()

    @pl.when(i == pl.num_programs(0) - 1)
    def _():
        zrow[...] = jnp.zeros(zrow.shape, i32)

        def per_expert(e, c):
            lo, hi = lo_ref[e], hi_ref[e]

            def zstart(r, c2):
                _row_copy(zrow.at[pl.ds(0, 1)], xs_out.at[pl.ds(r, 1)], sem_pad).start()
                return c2

            def zwait(r, c2):
                _row_copy(zrow.at[pl.ds(0, 1)], xs_out.at[pl.ds(0, 1)], sem_pad).wait()
                return c2

            lax.fori_loop(lo, hi, zstart, 0)
            lax.fori_loop(lo, hi, zwait, 0)
            return c

        lax.fori_loop(0, N_EXPERTS, per_expert, 0)

        rows = zrow.shape[0]
        first = hi_ref[N_EXPERTS - 1] // rows
        last = xs_out.shape[0] // rows

        def bstart(b, c2):
            _row_copy(zrow, xs_out.at[pl.ds(pl.multiple_of(b * rows, rows), rows)], sem_pad).start()
            return c2

        def bwait(b, c2):
            _row_copy(zrow, xs_out.at[pl.ds(0, rows)], sem_pad).wait()
            return c2

        lax.fori_loop(first, last, bstart, 0)
        lax.fori_loop(first, last, bwait, 0)


def _dispatch(pad_lo, pad_hi, dest2d, hp, n_rows):
    m, dh = hp.shape
    tm = TOK_BLK
    grid_spec = pltpu.PrefetchScalarGridSpec(
        num_scalar_prefetch=2,
        grid=(m // tm,),
        in_specs=[pl.BlockSpec(memory_space=pl.ANY),
                  pl.BlockSpec((tm, dh), lambda i, lo, hi: (i, 0))],
        out_specs=pl.BlockSpec(memory_space=pl.ANY),
        scratch_shapes=[pltpu.SMEM((tm * TOP_K,), i32), pltpu.VMEM((MOE_ROWS, dh), i32),
                        pltpu.SemaphoreType.DMA, pltpu.SemaphoreType.DMA, pltpu.SemaphoreType.DMA],
    )
    return pl.pallas_call(
        _dispatch_kernel,
        out_shape=jax.ShapeDtypeStruct((n_rows, dh), i32),
        grid_spec=grid_spec,
        compiler_params=_cparams(("arbitrary",)),
        name="moe_dispatch",
    )(pad_lo, pad_hi, dest2d, hp)


def _expert_kernel(be_ref, nu_ref, x_ref, wi_ref, bi_ref, wo_ref, bo_ref, y_ref, wib_ref, wob_ref):
    i = pl.program_id(0)

    @pl.when(jnp.logical_or(i == 0, be_ref[i] != be_ref[jnp.maximum(i - 1, 0)]))
    def _():
        wib_ref[...] = wi_ref[...].astype(bf16)
        wob_ref[...] = wo_ref[...].astype(bf16)

    @pl.when(i >= nu_ref[0])
    def _():
        y_ref[...] = jnp.zeros(y_ref.shape, f32)

    @pl.when(i < nu_ref[0])
    def _():
        xb = _unpack_bf16_pairs(x_ref[...])
        hh = jnp.dot(xb, wib_ref[...], preferred_element_type=f32) + bi_ref[...]
        gate = jnp.minimum(hh[:, :D_FF], SWIGLU_LIMIT)
        up = jnp.clip(hh[:, D_FF:], -SWIGLU_LIMIT, SWIGLU_LIMIT)
        act = gate * (1.0 / (1.0 + jnp.exp(-SWIGLU_ALPHA * gate))) * (up + 1.0)
        y_ref[...] = jnp.dot(act.astype(bf16), wob_ref[...], preferred_element_type=f32) + bo_ref[...]


def _experts(block_e, n_used, xs, w_in, b_in, w_out, b_out, layer):
    n_rows, dh = xs.shape
    d = 2 * dh
    nb = n_rows // MOE_ROWS
    blk = lambda i, be, nu: (jnp.minimum(i, nu[0] - 1), 0)
    per_e = lambda i, be, nu: (be[i], 0, 0)
    per_le = lambda i, be, nu: (layer, be[i], 0, 0)
    grid_spec = pltpu.PrefetchScalarGridSpec(
        num_scalar_prefetch=2,
        grid=(nb,),
        in_specs=[pl.BlockSpec((MOE_ROWS, dh), blk),
                  pl.BlockSpec((None, None, d, 2 * D_FF), per_le),
                  pl.BlockSpec((None, 1, 2 * D_FF), per_e),
                  pl.BlockSpec((None, None, D_FF, d), per_le),
                  pl.BlockSpec((None, 1, d), per_e)],
        out_specs=pl.BlockSpec((MOE_ROWS, d), lambda i, be, nu: (i, 0)),
        scratch_shapes=[pltpu.VMEM((d, 2 * D_FF), bf16), pltpu.VMEM((D_FF, d), bf16)],
    )
    return pl.pallas_call(
        _expert_kernel,
        out_shape=jax.ShapeDtypeStruct((n_rows, d), f32),
        grid_spec=grid_spec,
        compiler_params=_cparams(("arbitrary",)),
        name="moe_experts",
    )(block_e, n_used, xs, w_in, b_in, w_out, b_out)


def _combine_kernel(dest_hbm, y_hbm, gate_ref, h_ref, g_ref, b_ref, h2_ref, h2b_ref,
                    dest_smem, ybuf, sem_idx, sem_rows, *, alpha):
    i = pl.program_id(0)
    cp = pltpu.make_async_copy(dest_hbm.at[i], dest_smem, sem_idx)
    cp.start()
    cp.wait()
    tm = h_ref.shape[0]

    def issue(g, c):
        base = pl.multiple_of(g * SUBLANE, SUBLANE)
        for s in range(SUBLANE):
            for k in range(TOP_K):
                d = dest_smem[k * tm + base + s]
                _row_copy(y_hbm.at[pl.ds(d, 1)], ybuf.at[k, pl.ds(base + s, 1)], sem_rows).start()
        return c

    lax.fori_loop(0, tm // SUBLANE, issue, 0)
    for k in range(TOP_K):
        _row_copy(y_hbm.at[pl.ds(0, tm)], ybuf.at[k], sem_rows).wait()
    gate = gate_ref[...]
    f = gate[:, 0:1] * ybuf[0]
    for k in range(1, TOP_K):
        f = f + gate[:, k:k + 1] * ybuf[k]
    hn = _layer_norm(alpha * h_ref[...] + f, g_ref[...], b_ref[...])
    h2_ref[...] = hn
    h2b_ref[...] = hn.astype(bf16)


def _combine(dest2d, y, gate, h, g, b, alpha):
    m, d = h.shape
    tm = TOK_BLK
    row = lambda i: (i, 0)
    const = lambda i: (0, 0)
    return pl.pallas_call(
        functools.partial(_combine_kernel, alpha=alpha),
        out_shape=[jax.ShapeDtypeStruct((m, d), f32), jax.ShapeDtypeStruct((m, d), bf16)],
        grid=(m // tm,),
        in_specs=[pl.BlockSpec(memory_space=pl.ANY),
                  pl.BlockSpec(memory_space=pl.ANY),
                  pl.BlockSpec((tm, LANE), row),
                  pl.BlockSpec((tm, d), row),
                  pl.BlockSpec((1, d), const),
                  pl.BlockSpec((1, d), const)],
        out_specs=[pl.BlockSpec((tm, d), row), pl.BlockSpec((tm, d), row)],
        scratch_shapes=[pltpu.SMEM((tm * TOP_K,), i32), pltpu.VMEM((TOP_K, tm, d), f32),
                        pltpu.SemaphoreType.DMA, pltpu.SemaphoreType.DMA],
        compiler_params=_cparams(("arbitrary",)),
        name="moe_combine_ln",
    )(dest2d, y, gate, h, g, b)


def _route(idx, cnt, m):
    counts = cnt[0, :N_EXPERTS].astype(i32)
    padded = (counts + MOE_ROWS - 1) // MOE_ROWS * MOE_ROWS
    padded_end = jnp.cumsum(padded)
    start_padded = padded_end - padded
    n_blocks = -(-(m * TOP_K) // MOE_ROWS) + N_EXPERTS
    first_row = jnp.arange(n_blocks, dtype=i32) * MOE_ROWS
    block_e = jnp.minimum(jnp.sum((first_row[:, None] >= padded_end[None, :]).astype(i32), axis=1), N_EXPERTS - 1)
    n_used = (padded_end[-1:] // MOE_ROWS).astype(i32)
    start = jnp.pad(start_padded.astype(f32), (0, LANE - N_EXPERTS))[None]
    dest = _dest_rows(idx, start)
    dest2d = dest[:, :TOP_K, :].reshape(m // TOK_BLK, TOP_K * TOK_BLK)
    return dest2d, block_e, n_used, start_padded + counts, padded_end, n_blocks * MOE_ROWS


def _moe(hp, h1, gate, idx, cnt, w_in, b_in, w_out, b_out, layer, g, b, alpha):
    m = h1.shape[0]
    dest2d, block_e, n_used, pad_lo, pad_hi, n_rows = _route(idx, cnt, m)
    xs = _dispatch(pad_lo, pad_hi, dest2d, hp, n_rows)
    y = _experts(block_e, n_used, xs, w_in, b_in, w_out, b_out, layer)
    return _combine(dest2d, y, gate, h1, g, b, alpha)


OD_Q = HC * C_DH
VT_ROWS = C_DH + 2 * SUBLANE
LOG2E = math.log2(math.e)


def _odd_proj_kernel(x_ref, w_ref, wv_ref, ww_ref, cq_ref, ck_ref, iq_ref, ik_ref, vt_ref, wt_ref):
    x = x_ref[...]
    acc = jnp.dot(x, w_ref[...], preferred_element_type=f32)
    cq_ref[...] = (acc[:, :OD_Q] * (C_DH ** -0.5 * LOG2E)).astype(bf16)
    ck_ref[...] = acc[:, OD_Q:OD_Q + C_DH].astype(bf16)
    o = OD_Q + C_DH
    iq_ref[...] = (acc[:, o:o + IDX_H * LANE] * (IDX_DH ** -0.5)).astype(bf16)
    ik_ref[...] = acc[:, o + IDX_H * LANE:].astype(bf16)
    vt = _nt(wv_ref[...], x).astype(bf16)
    extra = (VT_ROWS - C_DH, vt.shape[1])
    ones_row = jnp.where(lax.broadcasted_iota(i32, extra, 0) == 0, 1.0, 0.0).astype(bf16)
    vt = jnp.concatenate([vt, ones_row], axis=0)
    for c in range(vt_ref.shape[0]):
        vt_ref[c] = vt[:, c * LANE:(c + 1) * LANE]
    wt_ref[...] = _nt(ww_ref[...], x)[:IDX_H] * (IDX_H ** -0.5)


def _odd_proj(hb, w_main, w_v_t, w_w_t, n):
    m, d = hb.shape
    tm = _pick(n, (640, 512, 256, 128))
    row = lambda i: (i, 0)
    const = lambda i: (0, 0)
    outs = [jax.ShapeDtypeStruct((m, OD_Q), bf16), jax.ShapeDtypeStruct((m, C_DH), bf16),
            jax.ShapeDtypeStruct((m, IDX_H * LANE), bf16), jax.ShapeDtypeStruct((m, LANE), bf16),
            jax.ShapeDtypeStruct((m // LANE, VT_ROWS, LANE), bf16), jax.ShapeDtypeStruct((IDX_H, m), f32)]
    return pl.pallas_call(
        _odd_proj_kernel,
        out_shape=outs,
        grid=(m // tm,),
        in_specs=[pl.BlockSpec((tm, d), row), pl.BlockSpec(w_main.shape, const),
                  pl.BlockSpec(w_v_t.shape, const), pl.BlockSpec(w_w_t.shape, const)],
        out_specs=[pl.BlockSpec((tm, OD_Q), row), pl.BlockSpec((tm, C_DH), row),
                   pl.BlockSpec((tm, IDX_H * LANE), row), pl.BlockSpec((tm, LANE), row),
                   pl.BlockSpec((tm // LANE, VT_ROWS, LANE), lambda i: (i, 0, 0)),
                   pl.BlockSpec((IDX_H, tm), lambda i: (0, i))],
        compiler_params=_cparams(("parallel",)),
        name="odd_proj",
    )(hb, w_main, w_v_t, w_w_t)


DSA_T = 128
IDX_BITS = 14


DSA_SPAN = 4
NO_CUT = 2 ** 30
HALF16 = 2 ** 15


def _for_blocks(lo, hi, fn):
    def body(i, c):
        fn(lo + 2 * i, 2)
        return c

    lax.fori_loop(0, (hi - lo) >> 1, body, 0)

    @pl.when(((hi - lo) & 1) == 1)
    def _():
        fn(hi - 1, 1)


def _pipelined_blocks(hi, qk, use, sta, stb):
    nquad = hi >> 2

    @pl.when(nquad > 0)
    def _():
        sta[...] = qk(0, 2)

    def body(i, c):
        j = 4 * i
        stb[...] = qk(j + 2, 2)
        use(j, 2, sta)
        sta[...] = qk(4 * jnp.minimum(i + 1, nquad - 1), 2)
        use(j + 2, 2, stb)
        return c

    lax.fori_loop(0, nquad, body, 0)
    _for_blocks(4 * nquad, hi, lambda j, nb: use(j, nb, None))


def _dsa_kernel(cq_ref, iq_ref, wt_ref, ik_ref, ck_ref, vt_ref, tile_ref, o_ref,
                ks_ref, lim_ref, acc_ref, m_ref, sta_ref, stb_ref, hi_ref, lo_ref, *, nblk, k_sel):
    t = DSA_T
    qi = pl.program_id(1)
    nkb = jnp.minimum(qi + 2, nblk)
    kf = float(k_sel)

    def key_pos(off, rows):
        return off + lax.broadcasted_iota(i32, (rows, t), 0)

    iq = iq_ref[...]
    iq_stack = jnp.concatenate([iq[:, h * LANE:(h + 1) * LANE] for h in range(IDX_H)], axis=0)
    wt = wt_ref[...]

    def score_qk(j, nb):
        return _nt(ik_ref[pl.ds(pl.multiple_of(j * t, t), nb * t), :], iq_stack)

    def score_span(j, nb, masked, st=None):
        rows = nb * t
        off = pl.multiple_of(j * t, t)
        if st is None:
            st = score_qk(j, nb)
        sc = jnp.maximum(st[:, :t], 0.0) * wt[0:1, :]
        for h in range(1, IDX_H):
            sc = sc + jnp.maximum(st[:, h * t:(h + 1) * t], 0.0) * wt[h:h + 1, :]
        sc = jnp.where(sc == 0.0, 0.0, sc)
        bits = lax.bitcast_convert_type(sc, i32)
        key = jnp.where(bits < 0, bits ^ 0x7FFFFFFF, bits)
        if masked:
            tpos = qi * t + lax.broadcasted_iota(i32, (rows, t), 1)
            adm = ((key_pos(off, rows) + CID_SHIFT) >> CHUNK_LOG2) <= ((tpos + CID_SHIFT) >> CHUNK_LOG2)
            key = jnp.where(adm, key, INT_MIN)
        ks_ref[pl.ds(off, rows), :] = key
        hi_ref[pl.ds(off, rows), :] = (key >> 16).astype(i16)
        lo_ref[pl.ds(off, rows), :] = ((key & 0xFFFF) - HALF16).astype(i16)

    _pipelined_blocks(qi, score_qk, lambda j, nb, st: score_span(j, nb, False, st), sta_ref, stb_ref)
    _for_blocks(qi, nkb, lambda j, nb: score_span(j, nb, True))
    pad = pl.ds(pl.multiple_of(nkb * t, t), (DSA_SPAN - 1) * t)
    ks_ref[pad, :] = jnp.full(((DSA_SPAN - 1) * t, t), INT_MIN, i32)
    hi_ref[pad, :] = jnp.full(((DSA_SPAN - 1) * t, t), -HALF16, i16)
    lo_ref[pad, :] = jnp.full(((DSA_SPAN - 1) * t, t), -HALF16, i16)

    span_rows = DSA_SPAN * t
    nspan = (nkb + DSA_SPAN - 1) // DSA_SPAN

    def count16(ref, pred_fn):
        lanes = 8 * SUBLANE

        def body(i, cnt):
            off = pl.multiple_of(i * span_rows, span_rows)
            hit = jnp.where(pred_fn(ref[pl.ds(off, span_rows), :]), jnp.int16(1), jnp.int16(0))
            parts = [hit[g * lanes:(g + 1) * lanes] for g in range(span_rows // lanes)]
            while len(parts) > 1:
                parts = [a + b for a, b in zip(parts[0::2], parts[1::2])]
            return cnt + parts[0]
        cnt = lax.fori_loop(0, nspan, body, jnp.zeros((lanes, t), i16))
        return jnp.sum(cnt.astype(i32).astype(f32), axis=0, keepdims=True)

    def search16(ref, cnt0, target):
        def cond(st):
            b, _, cnt = st
            return jnp.logical_and(b < 16, jnp.max(cnt - target) > 0.0)

        def body(st):
            b, d, cnt = st
            for _ in range(4):
                cand = d + lax.shift_left(jnp.int32(1), 15 - b)
                cand16 = cand.astype(i16)
                c = count16(ref, lambda v: v >= cand16)
                ok = c >= target
                b, d, cnt = b + 1, jnp.where(ok, cand, d), jnp.where(ok, c, cnt)
            return b, d, cnt

        _, d, cnt = lax.while_loop(cond, body, (jnp.int32(0), jnp.full((1, t), -HALF16, i32), cnt0))
        return d, cnt

    target = jnp.full((1, t), kf, f32)
    n_adm = count16(hi_ref, lambda v: v > jnp.int16(-HALF16))
    d_hi, cnt_hi = search16(hi_ref, n_adm, target)
    d_hi16 = d_hi.astype(i16)
    n_above = count16(hi_ref, lambda v: v > d_hi16)

    def keep_bucket(i, c):
        off = pl.multiple_of(i * span_rows, span_rows)
        sl = pl.ds(off, span_rows)
        lo_ref[sl, :] = jnp.where(hi_ref[sl, :] == d_hi16, lo_ref[sl, :], jnp.int16(-HALF16))
        return c

    lax.fori_loop(0, nspan, keep_bucket, 0)
    d_lo, cnt_lo = search16(lo_ref, cnt_hi - n_above, target - n_above)
    thr = lax.shift_left(d_hi, 16) | ((d_lo + HALF16) & 0xFFFF)
    cnt = n_above + cnt_lo

    def count(pred_fn):
        lanes = 8 * SUBLANE

        def body(i, cnt):
            off = pl.multiple_of(i * span_rows, span_rows)
            hit = jnp.where(pred_fn(ks_ref[pl.ds(off, span_rows), :], off), 1.0, 0.0)
            return cnt + jnp.sum(hit.reshape(span_rows // lanes, lanes, t), axis=0)
        cnt = lax.fori_loop(0, nspan, body, jnp.zeros((lanes, t), f32))
        return jnp.sum(cnt, axis=0, keepdims=True)

    thr = jnp.maximum(thr, INT_MIN + 1)

    lim_ref[...] = jnp.full(lim_ref.shape, NO_CUT, i32)

    @pl.when(jnp.max(cnt) > kf)
    def _():
        need = kf - count(lambda kk, off: kk > thr)

        def idx_step(b, lim):
            cand = lim + lax.shift_left(jnp.int32(1), IDX_BITS - 1 - b)
            c = count(lambda kk, off: jnp.logical_and(kk == thr, key_pos(off, span_rows) < cand))
            return jnp.where(c <= need, cand, lim)

        lim = lax.fori_loop(0, IDX_BITS, idx_step, jnp.zeros((1, t), i32))
        lim_ref[0:1, :] = jnp.where(cnt > kf, lim, NO_CUT)

    lim = lim_ref[0:1, :]

    cq = cq_ref[...]
    q_stack = jnp.concatenate([cq[:, h * LANE:(h + 1) * LANE] for h in range(HC)], axis=0)
    m_ref[...] = jnp.full(m_ref.shape, NEG, f32)
    acc_ref[...] = jnp.zeros(acc_ref.shape, f32)

    def attn_qk(j, nb):
        return _nt(ck_ref[pl.ds(pl.multiple_of(j * t, t), nb * t), :], q_stack)

    def attn_span(j, nb, near, st=None):
        rows = nb * t
        off = pl.multiple_of(j * t, t)
        kk = ks_ref[pl.ds(off, rows), :]
        sel = kk >= jnp.where(key_pos(off, rows) < lim, thr, thr + 1)
        if st is None:
            st = attn_qk(j, nb)
        m_old = m_ref[...]
        ps, m_news = [], []
        for h in range(HC):
            s = st[:, h * t:(h + 1) * t]
            if near:
                s = s + tile_ref[j - qi + 1, h]
            s = jnp.where(sel, s, NEG)
            m_new = jnp.maximum(m_old[:, h * t:(h + 1) * t], jnp.max(s, axis=0, keepdims=True))
            ps.append(jnp.exp2(s - m_new))
            m_news.append(m_new)
        p = jnp.concatenate(ps, axis=1)
        m_new = jnp.concatenate(m_news, axis=1)
        alpha = jnp.exp2(m_old - m_new)
        vt = vt_ref[j] if nb == 1 else jnp.concatenate([vt_ref[j], vt_ref[j + 1]], axis=1)
        acc_ref[...] = alpha * acc_ref[...] + jnp.dot(vt, p.astype(bf16), preferred_element_type=f32)
        m_ref[...] = m_new

    def near_body(j, c):
        attn_span(j, 1, True)
        return c

    n_far = jnp.maximum(qi - 1, 0)
    _pipelined_blocks(n_far, attn_qk, lambda j, nb, st: attn_span(j, nb, False, st), sta_ref, stb_ref)
    lax.fori_loop(n_far, nkb, near_body, 0)
    out_t = acc_ref[0:C_DH, :] / acc_ref[C_DH:C_DH + 1, :]
    for h in range(HC):
        o_ref[:, h * C_DH:(h + 1) * C_DH] = out_t[:, h * t:(h + 1) * t].T.astype(bf16)


def _dsa_attention(cq, ck, iq, ik, vt, wt, tiles, bsz, n, k_sel):
    m = cq.shape[0]
    t = DSA_T
    nblk = n // t
    vt4 = vt.reshape(bsz, nblk, VT_ROWS, LANE)
    return pl.pallas_call(
        functools.partial(_dsa_kernel, nblk=nblk, k_sel=k_sel),
        out_shape=jax.ShapeDtypeStruct((m, HC * C_DH), bf16),
        grid=(bsz, nblk),
        in_specs=[pl.BlockSpec((t, HC * C_DH), lambda b, i: (b * nblk + i, 0)),
                  pl.BlockSpec((t, IDX_H * LANE), lambda b, i: (b * nblk + i, 0)),
                  pl.BlockSpec((IDX_H, t), lambda b, i: (0, b * nblk + i)),
                  pl.BlockSpec((n, LANE), lambda b, i: (b, 0)),
                  pl.BlockSpec((n, C_DH), lambda b, i: (b, 0)),
                  pl.BlockSpec((None, nblk, VT_ROWS, LANE), lambda b, i: (b, 0, 0, 0)),
                  pl.BlockSpec(tiles.shape, lambda b, i: (0, 0, 0, 0))],
        out_specs=pl.BlockSpec((t, HC * C_DH), lambda b, i: (b * nblk + i, 0)),
        scratch_shapes=[pltpu.VMEM((n + (DSA_SPAN - 1) * t, t), i32), pltpu.VMEM((SUBLANE, t), i32),
                        pltpu.VMEM((VT_ROWS, HC * t), f32), pltpu.VMEM((1, HC * t), f32),
                        pltpu.VMEM((2 * t, HC * t), f32), pltpu.VMEM((2 * t, HC * t), f32),
                        pltpu.VMEM((n + (DSA_SPAN - 1) * t, t), i16), pltpu.VMEM((n + (DSA_SPAN - 1) * t, t), i16)],
        compiler_params=_cparams(("parallel", "parallel")),
        name="dsa_attention",
    )(cq, iq, wt, ik, ck, vt4, tiles)


def _rot_half_cols(w):
    half = w.shape[-1] // 2
    return jnp.concatenate([-w[..., half:], w[..., :half]], axis=-1)


def _prep_even(w_in, w_uq, w_ukv):
    o0 = Q_LORA + KV_LORA
    kr = w_in[:, o0:o0 + A_ROPE]
    w_in2 = jnp.concatenate([w_in[:, :o0], w_in[:, o0 + A_ROPE:], kr, _rot_half_cols(kr)], axis=1).astype(bf16)
    wq = w_uq.reshape(Q_LORA, HA, A_NOPE + A_ROPE)
    wq_r = wq[:, :, A_NOPE:]
    wq2 = jnp.concatenate([wq[:, :, :A_NOPE], wq_r, _rot_half_cols(wq_r)], axis=-1).reshape(Q_LORA, HA * 2 * LANE)
    wkv = w_ukv.reshape(KV_LORA, HA, A_NOPE + A_V)
    wkv2 = jnp.concatenate([wkv[:, :, :A_NOPE].reshape(KV_LORA, HA * A_NOPE),
                            wkv[:, :, A_NOPE:].reshape(KV_LORA, HA * A_V)], axis=1)
    return w_in2, wq2.astype(bf16), wkv2.astype(bf16)


def _prep_odd(w_in):
    d = w_in.shape[0]
    o = OD_Q
    w_q, w_k, w_v = w_in[:, :o], w_in[:, o:o + C_DH], w_in[:, o + C_DH:o + 2 * C_DH]
    o2 = o + 2 * C_DH
    w_iq = w_in[:, o2:o2 + IDX_H * IDX_DH].reshape(d, IDX_H, IDX_DH)
    w_iq = jnp.pad(w_iq, ((0, 0), (0, 0), (0, LANE - IDX_DH))).reshape(d, IDX_H * LANE)
    o3 = o2 + IDX_H * IDX_DH
    w_ik = jnp.pad(w_in[:, o3:o3 + IDX_DH], ((0, 0), (0, LANE - IDX_DH)))
    w_iw = w_in[:, o3 + IDX_DH:o3 + IDX_DH + IDX_H]
    w_main = jnp.concatenate([w_q, w_k, w_iq, w_ik], axis=1).astype(bf16)
    w_v_t = w_v.T.astype(bf16)
    w_w_t = jnp.pad(w_iw.T, ((0, 2 * SUBLANE - IDX_H), (0, 0))).astype(bf16)
    return w_main, w_v_t, w_w_t


def _t5_bucket(rel):
    half = N_BUCKETS // 2
    max_exact = half // 2
    base = jnp.where(rel > 0, half, 0)
    nn = jnp.abs(rel)
    nf = jnp.maximum(nn, 1).astype(f32)
    large = max_exact + (jnp.log(nf / max_exact) / math.log(MAX_DIST / max_exact)
                         * (half - max_exact)).astype(i32)
    large = jnp.minimum(large, half - 1)
    return base + jnp.where(nn < max_exact, nn, large)


def _bias_tiles(rel_table):
    t = DSA_T
    dlt = jnp.arange(3, dtype=i32)[:, None, None] - 1
    rel = dlt * t + jnp.arange(t, dtype=i32)[None, :, None] - jnp.arange(t, dtype=i32)[None, None, :]
    onehot = (_t5_bucket(rel)[..., None] == jnp.arange(N_BUCKETS, dtype=i32)).astype(f32)
    tiles = jnp.einsum('dstb,bh->dhst', onehot, rel_table.astype(f32), precision=lax.Precision.HIGHEST)
    far = rel_table.astype(f32)[N_BUCKETS // 2 - 1]
    return (tiles - far[None, :, None, None]) * LOG2E


def _rope_table(n):
    half = A_ROPE // 2
    inv = ROPE_THETA ** (-jnp.arange(half, dtype=f32) / half)
    ang = jnp.arange(n).astype(f32)[:, None] * inv[None, :]
    cos, sin = jnp.cos(ang), jnp.sin(ang)
    return jnp.concatenate([cos, cos, sin, sin], axis=1)


def _split_hi_lo(w):
    hi = w.astype(bf16)
    lo = (w - hi.astype(f32)).astype(bf16)
    return hi, lo


def kernel(x, meta_tokens, rel_bias_table, ev_w_in, mla_q_norm, mla_kv_norm, mla_w_uq, mla_w_ukv,
           ev_w_out, od_w_in, od_w_out, ln_g, ln_b, router_w, router_b, exp_w_in, exp_b_in,
           exp_w_out, exp_b_out):
    bsz, seq, d = x.shape
    depth = ln_g.shape[0]
    alpha = (2 * depth) ** 0.25
    n_tot = seq + N_META
    n = -(-n_tot // Q_BLOCK) * Q_BLOCK
    m = bsz * n
    assert m % TOK_BLK == 0 and d % (2 * LANE) == 0
    h = jnp.concatenate([jnp.broadcast_to(meta_tokens[None].astype(x.dtype), (bsz, N_META, d)), x,
                         jnp.zeros((bsz, n - n_tot, d), x.dtype)], axis=1).reshape(m, d)
    hb = h.astype(bf16)
    k_sel = min(TOPK_MAX, seq // 4)
    tab = _rope_table(n)
    tiles = _bias_tiles(rel_bias_table)
    for i in range(depth):
        j = i // 2
        if i % 2 == 0:
            w_in2, wq2, wkv2 = _prep_even(ev_w_in[j], mla_w_uq[j], mla_w_ukv[j])
            qa, ka, va, sbq, sbk, sbv = _even_proj(hb, w_in2, mla_q_norm[j][None], mla_kv_norm[j][None],
                                                   wq2, wkv2, tab, n)
            out_a = _mla_attention(qa, ka, va, bsz, n)
            out_b = _sb_attention(sbq, sbk, sbv, bsz, n)
            w_o = ev_w_out[j].astype(bf16)
            parts, weights = [out_a, out_b], [w_o[:HA * A_V], w_o[HA * A_V:]]
        else:
            w_main, w_v_t, w_w_t = _prep_odd(od_w_in[j])
            cq, ck, iq, ik, vt, wt = _odd_proj(hb, w_main, w_v_t, w_w_t, n)
            out_c = _dsa_attention(cq, ck, iq, ik, vt, wt, tiles, bsz, n, k_sel)
            parts, weights = [out_c], [od_w_out[j].astype(bf16)]
        rw = jnp.pad(router_w[i], ((0, 0), (0, LANE - N_EXPERTS)))
        rwh, rwl = _split_hi_lo(rw)
        rb = jnp.pad(router_b[i], (0, LANE - N_EXPERTS))[None]
        h1, hp, gate, idx, cnt = _mix_out(parts, weights, h, ln_g[i, 0][None], ln_b[i, 0][None], rwh, rwl, rb, alpha)
        h, hb = _moe(hp, h1, gate, idx, cnt, exp_w_in, exp_b_in[i][:, None, :],
                     exp_w_out, exp_b_out[i][:, None, :], i,
                     ln_g[i, 1][None], ln_b[i, 1][None], alpha)
    return h.reshape(bsz, n, d)[:, N_META:N_META + seq]
```

```python
import functools
import math

import numpy as np
import jax
import jax.numpy as jnp
from jax import lax
from jax.experimental import pallas as pl
from jax.experimental.pallas import tpu as pltpu

f32 = jnp.float32
bf16 = jnp.bfloat16
i32 = jnp.int32

CHUNK = 64
N_META = 16
Q_BLOCK = 128
HA, A_NOPE, A_ROPE, A_V = 4, 128, 64, 128
Q_LORA, KV_LORA = 512, 256
ROPE_THETA = 10000.0
HB, B_DH = 4, 128
HC, C_DH = 8, 128
IDX_H, IDX_DH = 8, 64
TOPK_MAX = 256
N_BUCKETS, MAX_DIST = 32, 128
N_EXPERTS, TOP_K, D_FF = 32, 4, 512
SWIGLU_LIMIT, SWIGLU_ALPHA = 7.0, 1.702
LN_EPS, RMS_EPS = 1e-5, 1e-6

LANE = 128
SUBLANE = 8
VMEM_LIMIT = 56 * 1024 * 1024

NEG = -1e30
INT_MIN = -(2 ** 31)
CID_SHIFT = CHUNK - N_META
CHUNK_LOG2 = 6
MOE_ROWS = 256
TOK_BLK = 512
SB_EXIT = -110.0


def _cparams(sem):
    return pltpu.CompilerParams(dimension_semantics=sem, vmem_limit_bytes=VMEM_LIMIT)


def _pick(n, cands):
    for c in cands:
        if n % c == 0:
            return c
    raise ValueError(f"no block size for {n}")


def _nt(a, b):
    return lax.dot_general(a, b, (((1,), (1,)), ((), ())), preferred_element_type=f32)


def _rms(x, g):
    return x * lax.rsqrt(jnp.mean(x * x, axis=-1, keepdims=True) + RMS_EPS) * g


EV_N = Q_LORA + KV_LORA + 3 * HB * B_DH + 2 * A_ROPE


def _rope_pair(block, tab, keep_low_only):
    prod = block * tab
    r = prod + pltpu.roll(prod, A_ROPE, 1)
    if keep_low_only:
        lane = lax.broadcasted_iota(i32, r.shape, 1)
        r = jnp.where(lane < A_ROPE, r, 0.0)
    return r


def _even_proj_kernel(x_ref, w_ref, qn_ref, kvn_ref, wuq_ref, wukv_ref, tab_ref,
                      qa_ref, ka_ref, va_ref, sbq_ref, sbk_ref, sbv_ref):
    acc = jnp.dot(x_ref[...], w_ref[...], preferred_element_type=f32)
    o = Q_LORA + KV_LORA
    hb = HB * B_DH
    sbq_ref[...] = (acc[:, o:o + hb] * (B_DH ** -0.5)).astype(bf16)
    sbk_ref[...] = acc[:, o + hb:o + 2 * hb].astype(bf16)
    sbv_ref[...] = acc[:, o + 2 * hb:o + 3 * hb].astype(bf16)
    tab = tab_ref[...]
    k_rope = _rope_pair(acc[:, o + 3 * hb:o + 3 * hb + 2 * A_ROPE], tab, False).astype(bf16)
    qn = _rms(acc[:, :Q_LORA], qn_ref[...]).astype(bf16)
    kvn = _rms(acc[:, Q_LORA:o], kvn_ref[...]).astype(bf16)
    qraw = jnp.dot(qn, wuq_ref[...], preferred_element_type=f32)
    kva = jnp.dot(kvn, wukv_ref[...], preferred_element_type=f32)
    scale = (A_NOPE + A_ROPE) ** -0.5 * LOG2E
    for h in range(HA):
        b0 = h * 2 * LANE
        qa_ref[:, b0:b0 + LANE] = (qraw[:, b0:b0 + LANE] * scale).astype(bf16)
        qr = _rope_pair(qraw[:, b0 + LANE:b0 + 2 * LANE], tab, True)
        qa_ref[:, b0 + LANE:b0 + 2 * LANE] = (qr * scale).astype(bf16)
        ka_ref[:, b0:b0 + LANE] = kva[:, h * A_NOPE:(h + 1) * A_NOPE].astype(bf16)
        ka_ref[:, b0 + LANE:b0 + 2 * LANE] = k_rope
    va_ref[...] = kva[:, HA * A_NOPE:].astype(bf16)


def _even_proj(hb, w_in, q_norm, kv_norm, w_uq, w_ukv, tab, n):
    m, d = hb.shape
    tm = _pick(n, (640, 512, 256, 128))
    nb = n // tm
    const = lambda i: (0, 0)
    row = lambda i: (i, 0)
    outs = [jax.ShapeDtypeStruct((m, c), bf16) for c in
            (HA * 2 * LANE, HA * 2 * LANE, HA * A_V, HB * B_DH, HB * B_DH, HB * B_DH)]
    return pl.pallas_call(
        _even_proj_kernel,
        out_shape=outs,
        grid=(m // tm,),
        in_specs=[pl.BlockSpec((tm, d), row),
                  pl.BlockSpec(w_in.shape, const),
                  pl.BlockSpec(q_norm.shape, const),
                  pl.BlockSpec(kv_norm.shape, const),
                  pl.BlockSpec(w_uq.shape, const),
                  pl.BlockSpec(w_ukv.shape, const),
                  pl.BlockSpec((tm, LANE), lambda i: (i % nb, 0))],
        out_specs=[pl.BlockSpec((tm, s.shape[1]), row) for s in outs],
        compiler_params=_cparams(("parallel",)),
        name="even_proj",
    )(hb, w_in, q_norm, kv_norm, w_uq, w_ukv, tab)


MLA_HG = 2


def _mla_kernel(q_ref, k_ref, v_ref, o_ref, m_ref, l_ref, acc_ref, sa_ref, sb_ref, *, t, nblk):
    qi = pl.program_id(2)
    m_ref[...] = jnp.full(m_ref.shape, NEG, f32)
    l_ref[...] = jnp.zeros(l_ref.shape, f32)
    acc_ref[...] = jnp.zeros(acc_ref.shape, f32)
    qs = [q_ref[:, g * 2 * LANE:(g + 1) * 2 * LANE] for g in range(MLA_HG)]

    def scores(j, dst):
        off = pl.multiple_of(j * t, t)
        for g in range(MLA_HG):
            dst[g] = _nt(qs[g], k_ref[pl.ds(off, t), g * 2 * LANE:(g + 1) * 2 * LANE])

    def step(off, tk, masked, src=None):
        if masked:
            qpos = qi * t + lax.broadcasted_iota(i32, (t, tk), 0)
            kpos = off + lax.broadcasted_iota(i32, (t, tk), 1)
            vis = ((kpos + CID_SHIFT) >> CHUNK_LOG2) <= ((qpos + CID_SHIFT) >> CHUNK_LOG2)
        for g in range(MLA_HG):
            if src is None:
                s = _nt(qs[g], k_ref[pl.ds(off, tk), g * 2 * LANE:(g + 1) * 2 * LANE])
            else:
                s = src[g]
            if masked:
                s = jnp.where(vis, s, NEG)
            m_old = m_ref[g]
            m_new = jnp.maximum(m_old, jnp.max(s, axis=1, keepdims=True))
            alpha = jnp.exp2(m_old - m_new)
            p = jnp.exp2(s - m_new)
            l_ref[g] = alpha * l_ref[g] + jnp.sum(p, axis=1, keepdims=True)
            acc_ref[g] = alpha * acc_ref[g] + jnp.dot(p.astype(bf16), v_ref[pl.ds(off, tk), g * A_V:(g + 1) * A_V],
                                                      preferred_element_type=f32)
            m_ref[g] = m_new

    npair = qi >> 1

    @pl.when(npair > 0)
    def _():
        scores(0, sa_ref)

    def pair_body(i, c):
        j = 2 * i
        scores(j + 1, sb_ref)
        step(pl.multiple_of(j * t, t), t, False, sa_ref)
        scores(2 * jnp.minimum(i + 1, npair - 1), sa_ref)
        step(pl.multiple_of((j + 1) * t, t), t, False, sb_ref)
        return c

    lax.fori_loop(0, npair, pair_body, 0)

    @pl.when((qi & 1) == 1)
    def _():
        step(pl.multiple_of((qi - 1) * t, t), t, False)

    step(pl.multiple_of(qi * t, t), t, True)

    @pl.when(qi + 1 < nblk)
    def _():
        step(pl.multiple_of((qi + 1) * t, t), Q_BLOCK, True)

    for g in range(MLA_HG):
        o_ref[:, g * A_V:(g + 1) * A_V] = (acc_ref[g] / l_ref[g]).astype(bf16)


def _mla_attention(qa, ka, va, bsz, n):
    m = qa.shape[0]
    t = _pick(n, (640, 512, 256, 128))
    nblk = n // t
    hg = MLA_HG
    return pl.pallas_call(
        functools.partial(_mla_kernel, t=t, nblk=nblk),
        out_shape=jax.ShapeDtypeStruct((m, HA * A_V), bf16),
        grid=(bsz, HA // hg, nblk),
        in_specs=[pl.BlockSpec((t, hg * 2 * LANE), lambda b, h, i: (b * nblk + i, h)),
                  pl.BlockSpec((n, hg * 2 * LANE), lambda b, h, i: (b, h)),
                  pl.BlockSpec((n, hg * A_V), lambda b, h, i: (b, h))],
        out_specs=pl.BlockSpec((t, hg * A_V), lambda b, h, i: (b * nblk + i, h)),
        scratch_shapes=[pltpu.VMEM((hg, t, 1), f32), pltpu.VMEM((hg, t, 1), f32), pltpu.VMEM((hg, t, A_V), f32),
                        pltpu.VMEM((hg, t, t), f32), pltpu.VMEM((hg, t, t), f32)],
        compiler_params=_cparams(("parallel", "parallel", "parallel")),
        name="mla_attention",
    )(qa, ka, va)


SB_T = 128


def _sb_kernel(q_ref, k_ref, v_ref, u_ref, o_ref):
    t = SB_T
    qi = pl.program_id(1)
    u = u_ref[...]
    qs = [q_ref[:, g * B_DH:(g + 1) * B_DH] for g in range(HB)]

    def block(j, cs, accs, masked):
        off = pl.multiple_of(j * t, t)
        if masked:
            row = lax.broadcasted_iota(i32, (t, t), 0)
            col = lax.broadcasted_iota(i32, (t, t), 1)
            causal = col < row
        new_c, new_acc = [], []
        for g in range(HB):
            z = _nt(qs[g], k_ref[pl.ds(off, t), g * B_DH:(g + 1) * B_DH])
            lk = -(jnp.maximum(z, 0.0) + jnp.log(1.0 + jnp.exp(-jnp.abs(z))))
            if masked:
                lk = jnp.where(causal, lk, 0.0)
            hi = lk.astype(bf16)
            lo = (lk - hi.astype(f32)).astype(bf16)
            cum = jnp.dot(hi, u, preferred_element_type=f32) + jnp.dot(lo, u, preferred_element_type=f32)
            a = jnp.exp(z + cs[g] + cum)
            if masked:
                a = jnp.where(causal, a, 0.0)
            new_acc.append(accs[g] + jnp.dot(a.astype(bf16), v_ref[pl.ds(off, t), g * B_DH:(g + 1) * B_DH],
                                             preferred_element_type=f32))
            new_c.append(cs[g] + cum[:, 0:1])
        return tuple(new_c), tuple(new_acc)

    cs, accs = block(qi, (jnp.zeros((t, 1), f32),) * HB, (jnp.zeros((t, B_DH), f32),) * HB, True)

    def cond(st):
        j, cs, _ = st
        cmax = cs[0]
        for c in cs[1:]:
            cmax = jnp.maximum(cmax, c)
        return jnp.logical_and(j >= 0, jnp.max(cmax) > SB_EXIT)

    def body(st):
        j, cs, accs = st
        cs, accs = block(j, cs, accs, False)
        return j - 1, cs, accs

    _, _, accs = lax.while_loop(cond, body, (qi - 1, cs, accs))
    for g in range(HB):
        o_ref[:, g * B_DH:(g + 1) * B_DH] = accs[g].astype(bf16)


def _sb_attention(sbq, sbk, sbv, bsz, n):
    m = sbq.shape[0]
    t = SB_T
    nblk = n // t
    u = (np.arange(t)[:, None] >= np.arange(t)[None, :]).astype(np.float32)
    u = jnp.asarray(u, bf16)
    w = HB * B_DH
    return pl.pallas_call(
        _sb_kernel,
        out_shape=jax.ShapeDtypeStruct((m, w), bf16),
        grid=(bsz, nblk),
        in_specs=[pl.BlockSpec((t, w), lambda b, i: (b * nblk + i, 0)),
                  pl.BlockSpec((n, w), lambda b, i: (b, 0)),
                  pl.BlockSpec((n, w), lambda b, i: (b, 0)),
                  pl.BlockSpec((t, t), lambda b, i: (0, 0))],
        out_specs=pl.BlockSpec((t, w), lambda b, i: (b * nblk + i, 0)),
        compiler_params=_cparams(("parallel", "parallel")),
        name="stick_breaking_attention",
    )(sbq, sbk, sbv, u)


def _pack_bf16_pairs(x):
    half = x.shape[1] // 2
    xb = x.astype(bf16).astype(f32)
    lo = lax.bitcast_convert_type(xb[:, :half], i32)
    hi = lax.bitcast_convert_type(xb[:, half:], i32)
    return ((lo >> 16) & 0xFFFF) | (hi & (-65536))


def _unpack_bf16_pairs(w):
    lo = lax.bitcast_convert_type(w << 16, f32).astype(bf16)
    hi = lax.bitcast_convert_type(w & (-65536), f32).astype(bf16)
    return jnp.concatenate([lo, hi], axis=1)


def _layer_norm(y, g, b):
    mu = jnp.mean(y, axis=-1, keepdims=True)
    yc = y - mu
    var = jnp.mean(yc * yc, axis=-1, keepdims=True)
    return yc * lax.rsqrt(var + LN_EPS) * g + b


def _mix_out_kernel(*refs, nparts, alpha):
    a_refs = refs[:nparts]
    w_refs = refs[nparts:2 * nparts]
    h_ref, g_ref, b_ref, rwh_ref, rwl_ref, rb_ref = refs[2 * nparts:2 * nparts + 6]
    h1_ref, hp_ref, gate_ref, idx_ref, cnt_ref = refs[2 * nparts + 6:]
    mix = jnp.dot(a_refs[0][...], w_refs[0][...], preferred_element_type=f32)
    for a_ref, w_ref in zip(a_refs[1:], w_refs[1:]):
        mix = mix + jnp.dot(a_ref[...], w_ref[...], preferred_element_type=f32)
    hn = _layer_norm(alpha * h_ref[...] + mix, g_ref[...], b_ref[...])
    h1_ref[...] = hn
    hp_ref[...] = _pack_bf16_pairs(hn)
    x_hi = hn.astype(bf16)
    x_lo = (hn - x_hi.astype(f32)).astype(bf16)
    rwh = rwh_ref[...]
    logits = (jnp.dot(x_hi, rwh, preferred_element_type=f32) + jnp.dot(x_lo, rwh, preferred_element_type=f32)
              + jnp.dot(x_hi, rwl_ref[...], preferred_element_type=f32) + rb_ref[...])
    lane_i = lax.broadcasted_iota(i32, logits.shape, 1)
    lane = lane_i.astype(f32)
    cur = jnp.where(lane_i < N_EXPERTS, logits, -jnp.inf)
    vals, idxs = [], []
    for _ in range(TOP_K):
        mx = jnp.max(cur, axis=1, keepdims=True)
        ix = jnp.min(jnp.where(cur == mx, lane, float(LANE)), axis=1, keepdims=True)
        vals.append(mx)
        idxs.append(ix)
        cur = jnp.where(lane == ix, -jnp.inf, cur)
    es = [jnp.exp(v - vals[0]) for v in vals]
    den = es[0]
    for e in es[1:]:
        den = den + e
    gate = jnp.zeros(logits.shape, f32)
    idx = jnp.zeros(logits.shape, f32)
    picked = jnp.zeros(logits.shape, f32)
    for k in range(TOP_K):
        gate = jnp.where(lane_i == k, es[k] / den, gate)
        idx = jnp.where(lane_i == k, idxs[k], idx)
        picked = picked + jnp.where(lane == idxs[k], 1.0, 0.0)
    gate_ref[...] = gate
    idx_ref[...] = idx.astype(i32)

    @pl.when(pl.program_id(0) == 0)
    def _():
        cnt_ref[...] = jnp.zeros(cnt_ref.shape, f32)

    cnt_ref[0:1, :] = cnt_ref[0:1, :] + jnp.sum(picked, axis=0, keepdims=True)


def _mix_out(parts, weights, h, g, b, rwh, rwl, rb, alpha):
    m, d = h.shape
    tm = _pick(m, (256, 128))
    row = lambda i: (i, 0)
    const = lambda i: (0, 0)
    nparts = len(parts)
    outs = [jax.ShapeDtypeStruct((m, d), f32), jax.ShapeDtypeStruct((m, d // 2), i32),
            jax.ShapeDtypeStruct((m, LANE), f32), jax.ShapeDtypeStruct((m, LANE), i32),
            jax.ShapeDtypeStruct((SUBLANE, LANE), f32)]
    in_specs = ([pl.BlockSpec((tm, p.shape[1]), row) for p in parts]
                + [pl.BlockSpec(w.shape, const) for w in weights]
                + [pl.BlockSpec((tm, d), row), pl.BlockSpec((1, d), const), pl.BlockSpec((1, d), const),
                   pl.BlockSpec(rwh.shape, const), pl.BlockSpec(rwl.shape, const), pl.BlockSpec((1, LANE), const)])
    return pl.pallas_call(
        functools.partial(_mix_out_kernel, nparts=nparts, alpha=alpha),
        out_shape=outs,
        grid=(m // tm,),
        in_specs=in_specs,
        out_specs=([pl.BlockSpec((tm, s.shape[1]), row) for s in outs[:-1]]
                   + [pl.BlockSpec((SUBLANE, LANE), const)]),
        compiler_params=_cparams(("arbitrary",)),
        name="mix_out_ln_router",
    )(*parts, *weights, h, g, b, rwh, rwl, rb)


def _row_copy(src, dst, sem):
    return pltpu.make_async_copy(src, dst, sem)


def _dest_kernel(idx_ref, tri_ref, start_ref, dest_ref, carry_ref):
    @pl.when(pl.program_id(0) == 0)
    def _():
        carry_ref[...] = jnp.zeros(carry_ref.shape, f32)

    idx = idx_ref[...]
    lane = lax.broadcasted_iota(i32, idx.shape, 1)
    hits = [lane == idx[:, k:k + 1] for k in range(TOP_K)]
    onehot = jnp.where(hits[0], 1.0, 0.0)
    for hit in hits[1:]:
        onehot = onehot + jnp.where(hit, 1.0, 0.0)
    carry = carry_ref[0:1, :]
    base = jnp.dot(tri_ref[...], onehot.astype(bf16), preferred_element_type=f32) + carry + start_ref[...]
    dest = jnp.zeros(idx.shape, f32)
    for k in range(TOP_K):
        dest = jnp.where(lane == k, jnp.sum(jnp.where(hits[k], base, 0.0), axis=1, keepdims=True), dest)
    dest_ref[...] = dest.T[0:SUBLANE, :].astype(i32)
    carry_ref[0:1, :] = carry + jnp.sum(onehot, axis=0, keepdims=True)


def _dest_rows(idx, start):
    m = idx.shape[0]
    tm = TOK_BLK
    tri = jnp.asarray(np.tril(np.ones((tm, tm), np.float32), -1), bf16)
    return pl.pallas_call(
        _dest_kernel,
        out_shape=jax.ShapeDtypeStruct((m // tm, SUBLANE, tm), i32),
        grid=(m // tm,),
        in_specs=[pl.BlockSpec((tm, LANE), lambda i: (i, 0)),
                  pl.BlockSpec((tm, tm), lambda i: (0, 0)),
                  pl.BlockSpec((1, LANE), lambda i: (0, 0))],
        out_specs=pl.BlockSpec((None, SUBLANE, tm), lambda i: (i, 0, 0)),
        scratch_shapes=[pltpu.VMEM((SUBLANE, LANE), f32)],
        compiler_params=_cparams(("arbitrary",)),
        name="moe_dest_rows",
    )(idx, tri, start)


def _dispatch_kernel(lo_ref, hi_ref, dest_hbm, hp_ref, xs_out, dest_smem, zrow, sem_idx, sem_rows, sem_pad):
    i = pl.program_id(0)
    cp = pltpu.make_async_copy(dest_hbm.at[i], dest_smem, sem_idx)
    cp.start()
    cp.wait()
    tm = hp_ref.shape[0]

    def issue(g, c):
        base = pl.multiple_of(g * SUBLANE, SUBLANE)
        for s in range(SUBLANE):
            for k in range(TOP_K):
                d = dest_smem[k * tm + base + s]
                _row_copy(hp_ref.at[pl.ds(base + s, 1)], xs_out.at[pl.ds(d, 1)], sem_rows).start()
        return c

    lax.fori_loop(0, tm // SUBLANE, issue, 0)
    for k in range(TOP_K):
        _row_copy(hp_ref, xs_out.at[pl.ds(0, tm)], sem_rows).wait()

    @pl.when(i == pl.num_programs(0) - 1)
    def _():
        zrow[...] = jnp.zeros(zrow.shape, i32)

        def per_expert(e, c):
            lo, hi = lo_ref[e], hi_ref[e]

            def zstart(r, c2):
                _row_copy(zrow.at[pl.ds(0, 1)], xs_out.at[pl.ds(r, 1)], sem_pad).start()
                return c2

            def zwait(r, c2):
                _row_copy(zrow.at[pl.ds(0, 1)], xs_out.at[pl.ds(0, 1)], sem_pad).wait()
                return c2

            lax.fori_loop(lo, hi, zstart, 0)
            lax.fori_loop(lo, hi, zwait, 0)
            return c

        lax.fori_loop(0, N_EXPERTS, per_expert, 0)

        rows = zrow.shape[0]
        first = hi_ref[N_EXPERTS - 1] // rows
        last = xs_out.shape[0] // rows

        def bstart(b, c2):
            _row_copy(zrow, xs_out.at[pl.ds(pl.multiple_of(b * rows, rows), rows)], sem_pad).start()
            return c2

        def bwait(b, c2):
            _row_copy(zrow, xs_out.at[pl.ds(0, rows)], sem_pad).wait()
            return c2

        lax.fori_loop(first, last, bstart, 0)
        lax.fori_loop(first, last, bwait, 0)


def _dispatch(pad_lo, pad_hi, dest2d, hp, n_rows):
    m, dh = hp.shape
    tm = TOK_BLK
    grid_spec = pltpu.PrefetchScalarGridSpec(
        num_scalar_prefetch=2,
        grid=(m // tm,),
        in_specs=[pl.BlockSpec(memory_space=pl.ANY),
                  pl.BlockSpec((tm, dh), lambda i, lo, hi: (i, 0))],
        out_specs=pl.BlockSpec(memory_space=pl.ANY),
        scratch_shapes=[pltpu.SMEM((tm * TOP_K,), i32), pltpu.VMEM((MOE_ROWS, dh), i32),
                        pltpu.SemaphoreType.DMA, pltpu.SemaphoreType.DMA, pltpu.SemaphoreType.DMA],
    )
    return pl.pallas_call(
        _dispatch_kernel,
        out_shape=jax.ShapeDtypeStruct((n_rows, dh), i32),
        grid_spec=grid_spec,
        compiler_params=_cparams(("arbitrary",)),
        name="moe_dispatch",
    )(pad_lo, pad_hi, dest2d, hp)


def _expert_kernel(be_ref, nu_ref, x_ref, wi_ref, bi_ref, wo_ref, bo_ref, y_ref, wib_ref, wob_ref):
    i = pl.program_id(0)

    @pl.when(jnp.logical_or(i == 0, be_ref[i] != be_ref[jnp.maximum(i - 1, 0)]))
    def _():
        wib_ref[...] = wi_ref[...].astype(bf16)
        wob_ref[...] = wo_ref[...].astype(bf16)

    @pl.when(i >= nu_ref[0])
    def _():
        y_ref[...] = jnp.zeros(y_ref.shape, f32)

    @pl.when(i < nu_ref[0])
    def _():
        xb = _unpack_bf16_pairs(x_ref[...])
        hh = jnp.dot(xb, wib_ref[...], preferred_element_type=f32) + bi_ref[...]
        gate = jnp.minimum(hh[:, :D_FF], SWIGLU_LIMIT)
        up = jnp.clip(hh[:, D_FF:], -SWIGLU_LIMIT, SWIGLU_LIMIT)
        act = gate * (1.0 / (1.0 + jnp.exp(-SWIGLU_ALPHA * gate))) * (up + 1.0)
        y_ref[...] = jnp.dot(act.astype(bf16), wob_ref[...], preferred_element_type=f32) + bo_ref[...]


def _experts(block_e, n_used, xs, w_in, b_in, w_out, b_out, layer):
    n_rows, dh = xs.shape
    d = 2 * dh
    nb = n_rows // MOE_ROWS
    blk = lambda i, be, nu: (jnp.minimum(i, nu[0] - 1), 0)
    per_e = lambda i, be, nu: (be[i], 0, 0)
    per_le = lambda i, be, nu: (layer, be[i], 0, 0)
    grid_spec = pltpu.PrefetchScalarGridSpec(
        num_scalar_prefetch=2,
        grid=(nb,),
        in_specs=[pl.BlockSpec((MOE_ROWS, dh), blk),
                  pl.BlockSpec((None, None, d, 2 * D_FF), per_le),
                  pl.BlockSpec((None, 1, 2 * D_FF), per_e),
                  pl.BlockSpec((None, None, D_FF, d), per_le),
                  pl.BlockSpec((None, 1, d), per_e)],
        out_specs=pl.BlockSpec((MOE_ROWS, d), lambda i, be, nu: (i, 0)),
        scratch_shapes=[pltpu.VMEM((d, 2 * D_FF), bf16), pltpu.VMEM((D_FF, d), bf16)],
    )
    return pl.pallas_call(
        _expert_kernel,
        out_shape=jax.ShapeDtypeStruct((n_rows, d), f32),
        grid_spec=grid_spec,
        compiler_params=_cparams(("arbitrary",)),
        name="moe_experts",
    )(block_e, n_used, xs, w_in, b_in, w_out, b_out)


def _combine_kernel(dest_hbm, y_hbm, gate_ref, h_ref, g_ref, b_ref, h2_ref, h2b_ref,
                    dest_smem, ybuf, sem_idx, sem_rows, *, alpha):
    i = pl.program_id(0)
    cp = pltpu.make_async_copy(dest_hbm.at[i], dest_smem, sem_idx)
    cp.start()
    cp.wait()
    tm = h_ref.shape[0]

    def issue(g, c):
        base = pl.multiple_of(g * SUBLANE, SUBLANE)
        for s in range(SUBLANE):
            for k in range(TOP_K):
                d = dest_smem[k * tm + base + s]
                _row_copy(y_hbm.at[pl.ds(d, 1)], ybuf.at[k, pl.ds(base + s, 1)], sem_rows).start()
        return c

    lax.fori_loop(0, tm // SUBLANE, issue, 0)
    for k in range(TOP_K):
        _row_copy(y_hbm.at[pl.ds(0, tm)], ybuf.at[k], sem_rows).wait()
    gate = gate_ref[...]
    f = gate[:, 0:1] * ybuf[0]
    for k in range(1, TOP_K):
        f = f + gate[:, k:k + 1] * ybuf[k]
    hn = _layer_norm(alpha * h_ref[...] + f, g_ref[...], b_ref[...])
    h2_ref[...] = hn
    h2b_ref[...] = hn.astype(bf16)


def _combine(dest2d, y, gate, h, g, b, alpha):
    m, d = h.shape
    tm = TOK_BLK
    row = lambda i: (i, 0)
    const = lambda i: (0, 0)
    return pl.pallas_call(
        functools.partial(_combine_kernel, alpha=alpha),
        out_shape=[jax.ShapeDtypeStruct((m, d), f32), jax.ShapeDtypeStruct((m, d), bf16)],
        grid=(m // tm,),
        in_specs=[pl.BlockSpec(memory_space=pl.ANY),
                  pl.BlockSpec(memory_space=pl.ANY),
                  pl.BlockSpec((tm, LANE), row),
                  pl.BlockSpec((tm, d), row),
                  pl.BlockSpec((1, d), const),
                  pl.BlockSpec((1, d), const)],
        out_specs=[pl.BlockSpec((tm, d), row), pl.BlockSpec((tm, d), row)],
        scratch_shapes=[pltpu.SMEM((tm * TOP_K,), i32), pltpu.VMEM((TOP_K, tm, d), f32),
                        pltpu.SemaphoreType.DMA, pltpu.SemaphoreType.DMA],
        compiler_params=_cparams(("arbitrary",)),
        name="moe_combine_ln",
    )(dest2d, y, gate, h, g, b)


def _route(idx, cnt, m):
    counts = cnt[0, :N_EXPERTS].astype(i32)
    padded = (counts + MOE_ROWS - 1) // MOE_ROWS * MOE_ROWS
    padded_end = jnp.cumsum(padded)
    start_padded = padded_end - padded
    n_blocks = -(-(m * TOP_K) // MOE_ROWS) + N_EXPERTS
    first_row = jnp.arange(n_blocks, dtype=i32) * MOE_ROWS
    block_e = jnp.minimum(jnp.sum((first_row[:, None] >= padded_end[None, :]).astype(i32), axis=1), N_EXPERTS - 1)
    n_used = (padded_end[-1:] // MOE_ROWS).astype(i32)
    start = jnp.pad(start_padded.astype(f32), (0, LANE - N_EXPERTS))[None]
    dest = _dest_rows(idx, start)
    dest2d = dest[:, :TOP_K, :].reshape(m // TOK_BLK, TOP_K * TOK_BLK)
    return dest2d, block_e, n_used, start_padded + counts, padded_end, n_blocks * MOE_ROWS


def _moe(hp, h1, gate, idx, cnt, w_in, b_in, w_out, b_out, layer, g, b, alpha):
    m = h1.shape[0]
    dest2d, block_e, n_used, pad_lo, pad_hi, n_rows = _route(idx, cnt, m)
    xs = _dispatch(pad_lo, pad_hi, dest2d, hp, n_rows)
    y = _experts(block_e, n_used, xs, w_in, b_in, w_out, b_out, layer)
    return _combine(dest2d, y, gate, h1, g, b, alpha)


OD_Q = HC * C_DH
VT_ROWS = C_DH + 2 * SUBLANE
LOG2E = math.log2(math.e)


def _odd_proj_kernel(x_ref, w_ref, wv_ref, ww_ref, cq_ref, ck_ref, iq_ref, ik_ref, vt_ref, wt_ref):
    x = x_ref[...]
    acc = jnp.dot(x, w_ref[...], preferred_element_type=f32)
    cq_ref[...] = (acc[:, :OD_Q] * (C_DH ** -0.5 * LOG2E)).astype(bf16)
    ck_ref[...] = acc[:, OD_Q:OD_Q + C_DH].astype(bf16)
    o = OD_Q + C_DH
    iq_ref[...] = (acc[:, o:o + IDX_H * LANE] * (IDX_DH ** -0.5)).astype(bf16)
    ik_ref[...] = acc[:, o + IDX_H * LANE:].astype(bf16)
    vt = _nt(wv_ref[...], x).astype(bf16)
    extra = (VT_ROWS - C_DH, vt.shape[1])
    ones_row = jnp.where(lax.broadcasted_iota(i32, extra, 0) == 0, 1.0, 0.0).astype(bf16)
    vt = jnp.concatenate([vt, ones_row], axis=0)
    for c in range(vt_ref.shape[0]):
        vt_ref[c] = vt[:, c * LANE:(c + 1) * LANE]
    wt_ref[...] = _nt(ww_ref[...], x)[:IDX_H] * (IDX_H ** -0.5)


def _odd_proj(hb, w_main, w_v_t, w_w_t, n):
    m, d = hb.shape
    tm = _pick(n, (640, 512, 256, 128))
    row = lambda i: (i, 0)
    const = lambda i: (0, 0)
    outs = [jax.ShapeDtypeStruct((m, OD_Q), bf16), jax.ShapeDtypeStruct((m, C_DH), bf16),
            jax.ShapeDtypeStruct((m, IDX_H * LANE), bf16), jax.ShapeDtypeStruct((m, LANE), bf16),
            jax.ShapeDtypeStruct((m // LANE, VT_ROWS, LANE), bf16), jax.ShapeDtypeStruct((IDX_H, m), f32)]
    return pl.pallas_call(
        _odd_proj_kernel,
        out_shape=outs,
        grid=(m // tm,),
        in_specs=[pl.BlockSpec((tm, d), row), pl.BlockSpec(w_main.shape, const),
                  pl.BlockSpec(w_v_t.shape, const), pl.BlockSpec(w_w_t.shape, const)],
        out_specs=[pl.BlockSpec((tm, OD_Q), row), pl.BlockSpec((tm, C_DH), row),
                   pl.BlockSpec((tm, IDX_H * LANE), row), pl.BlockSpec((tm, LANE), row),
                   pl.BlockSpec((tm // LANE, VT_ROWS, LANE), lambda i: (i, 0, 0)),
                   pl.BlockSpec((IDX_H, tm), lambda i: (0, i))],
        compiler_params=_cparams(("parallel",)),
        name="odd_proj",
    )(hb, w_main, w_v_t, w_w_t)


DSA_T = 128
IDX_BITS = 14


DSA_SPAN = 4
NO_CUT = 2 ** 30


def _for_blocks(lo, hi, fn):
    def body(i, c):
        fn(lo + 2 * i, 2)
        return c

    lax.fori_loop(0, (hi - lo) >> 1, body, 0)

    @pl.when(((hi - lo) & 1) == 1)
    def _():
        fn(hi - 1, 1)


def _pipelined_blocks(hi, qk, use, sta, stb):
    nquad = hi >> 2

    @pl.when(nquad > 0)
    def _():
        sta[...] = qk(0, 2)

    def body(i, c):
        j = 4 * i
        stb[...] = qk(j + 2, 2)
        use(j, 2, sta)
        sta[...] = qk(4 * jnp.minimum(i + 1, nquad - 1), 2)
        use(j + 2, 2, stb)
        return c

    lax.fori_loop(0, nquad, body, 0)
    _for_blocks(4 * nquad, hi, lambda j, nb: use(j, nb, None))


def _dsa_kernel(cq_ref, iq_ref, wt_ref, ik_ref, ck_ref, vt_ref, tile_ref, o_ref,
                ks_ref, lim_ref, acc_ref, m_ref, sta_ref, stb_ref, *, nblk, k_sel):
    t = DSA_T
    qi = pl.program_id(1)
    nkb = jnp.minimum(qi + 2, nblk)
    kf = float(k_sel)

    def key_pos(off, rows):
        return off + lax.broadcasted_iota(i32, (rows, t), 0)

    iq = iq_ref[...]
    iq_stack = jnp.concatenate([iq[:, h * LANE:(h + 1) * LANE] for h in range(IDX_H)], axis=0)
    wt = wt_ref[...]

    def score_qk(j, nb):
        return _nt(ik_ref[pl.ds(pl.multiple_of(j * t, t), nb * t), :], iq_stack)

    def score_span(j, nb, masked, st=None):
        rows = nb * t
        off = pl.multiple_of(j * t, t)
        if st is None:
            st = score_qk(j, nb)
        sc = jnp.maximum(st[:, :t], 0.0) * wt[0:1, :]
        for h in range(1, IDX_H):
            sc = sc + jnp.maximum(st[:, h * t:(h + 1) * t], 0.0) * wt[h:h + 1, :]
        sc = jnp.where(sc == 0.0, 0.0, sc)
        bits = lax.bitcast_convert_type(sc, i32)
        key = jnp.where(bits < 0, bits ^ 0x7FFFFFFF, bits)
        if masked:
            tpos = qi * t + lax.broadcasted_iota(i32, (rows, t), 1)
            adm = ((key_pos(off, rows) + CID_SHIFT) >> CHUNK_LOG2) <= ((tpos + CID_SHIFT) >> CHUNK_LOG2)
            key = jnp.where(adm, key, INT_MIN)
        ks_ref[pl.ds(off, rows), :] = key

    _pipelined_blocks(qi, score_qk, lambda j, nb, st: score_span(j, nb, False, st), sta_ref, stb_ref)
    _for_blocks(qi, nkb, lambda j, nb: score_span(j, nb, True))
    ks_ref[pl.ds(pl.multiple_of(nkb * t, t), (DSA_SPAN - 1) * t), :] = jnp.full(((DSA_SPAN - 1) * t, t), INT_MIN, i32)

    span_rows = DSA_SPAN * t
    nspan = (nkb + DSA_SPAN - 1) // DSA_SPAN

    def count(pred_fn):
        lanes = 8 * SUBLANE

        def body(i, cnt):
            off = pl.multiple_of(i * span_rows, span_rows)
            hit = jnp.where(pred_fn(ks_ref[pl.ds(off, span_rows), :], off), 1.0, 0.0)
            return cnt + jnp.sum(hit.reshape(span_rows // lanes, lanes, t), axis=0)
        cnt = lax.fori_loop(0, nspan, body, jnp.zeros((lanes, t), f32))
        return jnp.sum(cnt, axis=0, keepdims=True)

    n_adm = count(lambda kk, off: kk > INT_MIN)

    def search_cond(st):
        b, _, cnt = st
        return jnp.logical_and(b < 32, jnp.max(cnt) > kf)

    def search_body(st):
        b, thr, cnt = st
        for _ in range(2):
            cand = thr + lax.shift_left(jnp.int32(1), 31 - b)
            c = count(lambda kk, off: kk >= cand)
            ok = c >= kf
            b, thr, cnt = b + 1, jnp.where(ok, cand, thr), jnp.where(ok, c, cnt)
        return b, thr, cnt

    _, thr, cnt = lax.while_loop(search_cond, search_body,
                                 (jnp.int32(0), jnp.full((1, t), INT_MIN, i32), n_adm))
    thr = jnp.maximum(thr, INT_MIN + 1)

    lim_ref[...] = jnp.full(lim_ref.shape, NO_CUT, i32)

    @pl.when(jnp.max(cnt) > kf)
    def _():
        need = kf - count(lambda kk, off: kk > thr)

        def idx_step(b, lim):
            cand = lim + lax.shift_left(jnp.int32(1), IDX_BITS - 1 - b)
            c = count(lambda kk, off: jnp.logical_and(kk == thr, key_pos(off, span_rows) < cand))
            return jnp.where(c <= need, cand, lim)

        lim = lax.fori_loop(0, IDX_BITS, idx_step, jnp.zeros((1, t), i32))
        lim_ref[0:1, :] = jnp.where(cnt > kf, lim, NO_CUT)

    lim = lim_ref[0:1, :]

    cq = cq_ref[...]
    q_stack = jnp.concatenate([cq[:, h * LANE:(h + 1) * LANE] for h in range(HC)], axis=0)
    m_ref[...] = jnp.full(m_ref.shape, NEG, f32)
    acc_ref[...] = jnp.zeros(acc_ref.shape, f32)

    def attn_qk(j, nb):
        return _nt(ck_ref[pl.ds(pl.multiple_of(j * t, t), nb * t), :], q_stack)

    def attn_span(j, nb, near, st=None):
        rows = nb * t
        off = pl.multiple_of(j * t, t)
        kk = ks_ref[pl.ds(off, rows), :]
        sel = kk >= jnp.where(key_pos(off, rows) < lim, thr, thr + 1)
        if st is None:
            st = attn_qk(j, nb)
        m_old = m_ref[...]
        ps, m_news = [], []
        for h in range(HC):
            s = st[:, h * t:(h + 1) * t]
            if near:
                ti = j - qi + 1
                s = s + (tile_ref[ti, h] if nb == 1 else
                         jnp.concatenate([tile_ref[ti, h], tile_ref[ti + 1, h]], axis=0))
            s = jnp.where(sel, s, NEG)
            m_new = jnp.maximum(m_old[:, h * t:(h + 1) * t], jnp.max(s, axis=0, keepdims=True))
            ps.append(jnp.exp2(s - m_new))
            m_news.append(m_new)
        p = jnp.concatenate(ps, axis=1)
        m_new = jnp.concatenate(m_news, axis=1)
        alpha = jnp.exp2(m_old - m_new)
        vt = vt_ref[j] if nb == 1 else jnp.concatenate([vt_ref[j], vt_ref[j + 1]], axis=1)
        acc_ref[...] = alpha * acc_ref[...] + jnp.dot(vt, p.astype(bf16), preferred_element_type=f32)
        m_ref[...] = m_new

    def near_body(j, c):
        attn_span(j, 1, True)
        return c

    n_far = jnp.maximum(qi - 1, 0)
    _pipelined_blocks(n_far, attn_qk, lambda j, nb, st: attn_span(j, nb, False, st), sta_ref, stb_ref)
    if nblk >= 2:
        attn_span(n_far, 2, True)

        @pl.when(n_far + 2 < nkb)
        def _():
            attn_span(n_far + 2, 1, True)
    else:
        lax.fori_loop(n_far, nkb, near_body, 0)
    out_t = acc_ref[0:C_DH, :] / acc_ref[C_DH:C_DH + 1, :]
    for h in range(HC):
        o_ref[:, h * C_DH:(h + 1) * C_DH] = out_t[:, h * t:(h + 1) * t].T.astype(bf16)


def _dsa_attention(cq, ck, iq, ik, vt, wt, tiles, bsz, n, k_sel):
    m = cq.shape[0]
    t = DSA_T
    nblk = n // t
    vt4 = vt.reshape(bsz, nblk, VT_ROWS, LANE)
    return pl.pallas_call(
        functools.partial(_dsa_kernel, nblk=nblk, k_sel=k_sel),
        out_shape=jax.ShapeDtypeStruct((m, HC * C_DH), bf16),
        grid=(bsz, nblk),
        in_specs=[pl.BlockSpec((t, HC * C_DH), lambda b, i: (b * nblk + i, 0)),
                  pl.BlockSpec((t, IDX_H * LANE), lambda b, i: (b * nblk + i, 0)),
                  pl.BlockSpec((IDX_H, t), lambda b, i: (0, b * nblk + i)),
                  pl.BlockSpec((n, LANE), lambda b, i: (b, 0)),
                  pl.BlockSpec((n, C_DH), lambda b, i: (b, 0)),
                  pl.BlockSpec((None, nblk, VT_ROWS, LANE), lambda b, i: (b, 0, 0, 0)),
                  pl.BlockSpec(tiles.shape, lambda b, i: (0, 0, 0, 0))],
        out_specs=pl.BlockSpec((t, HC * C_DH), lambda b, i: (b * nblk + i, 0)),
        scratch_shapes=[pltpu.VMEM((n + (DSA_SPAN - 1) * t, t), i32), pltpu.VMEM((SUBLANE, t), i32),
                        pltpu.VMEM((VT_ROWS, HC * t), f32), pltpu.VMEM((1, HC * t), f32),
                        pltpu.VMEM((2 * t, HC * t), f32), pltpu.VMEM((2 * t, HC * t), f32)],
        compiler_params=_cparams(("parallel", "parallel")),
        name="dsa_attention",
    )(cq, iq, wt, ik, ck, vt4, tiles)


def _rot_half_cols(w):
    half = w.shape[-1] // 2
    return jnp.concatenate([-w[..., half:], w[..., :half]], axis=-1)


def _prep_even(w_in, w_uq, w_ukv):
    o0 = Q_LORA + KV_LORA
    kr = w_in[:, o0:o0 + A_ROPE]
    w_in2 = jnp.concatenate([w_in[:, :o0], w_in[:, o0 + A_ROPE:], kr, _rot_half_cols(kr)], axis=1).astype(bf16)
    wq = w_uq.reshape(Q_LORA, HA, A_NOPE + A_ROPE)
    wq_r = wq[:, :, A_NOPE:]
    wq2 = jnp.concatenate([wq[:, :, :A_NOPE], wq_r, _rot_half_cols(wq_r)], axis=-1).reshape(Q_LORA, HA * 2 * LANE)
    wkv = w_ukv.reshape(KV_LORA, HA, A_NOPE + A_V)
    wkv2 = jnp.concatenate([wkv[:, :, :A_NOPE].reshape(KV_LORA, HA * A_NOPE),
                            wkv[:, :, A_NOPE:].reshape(KV_LORA, HA * A_V)], axis=1)
    return w_in2, wq2.astype(bf16), wkv2.astype(bf16)


def _prep_odd(w_in):
    d = w_in.shape[0]
    o = OD_Q
    w_q, w_k, w_v = w_in[:, :o], w_in[:, o:o + C_DH], w_in[:, o + C_DH:o + 2 * C_DH]
    o2 = o + 2 * C_DH
    w_iq = w_in[:, o2:o2 + IDX_H * IDX_DH].reshape(d, IDX_H, IDX_DH)
    w_iq = jnp.pad(w_iq, ((0, 0), (0, 0), (0, LANE - IDX_DH))).reshape(d, IDX_H * LANE)
    o3 = o2 + IDX_H * IDX_DH
    w_ik = jnp.pad(w_in[:, o3:o3 + IDX_DH], ((0, 0), (0, LANE - IDX_DH)))
    w_iw = w_in[:, o3 + IDX_DH:o3 + IDX_DH + IDX_H]
    w_main = jnp.concatenate([w_q, w_k, w_iq, w_ik], axis=1).astype(bf16)
    w_v_t = w_v.T.astype(bf16)
    w_w_t = jnp.pad(w_iw.T, ((0, 2 * SUBLANE - IDX_H), (0, 0))).astype(bf16)
    return w_main, w_v_t, w_w_t


def _t5_bucket(rel):
    half = N_BUCKETS // 2
    max_exact = half // 2
    base = jnp.where(rel > 0, half, 0)
    nn = jnp.abs(rel)
    nf = jnp.maximum(nn, 1).astype(f32)
    large = max_exact + (jnp.log(nf / max_exact) / math.log(MAX_DIST / max_exact)
                         * (half - max_exact)).astype(i32)
    large = jnp.minimum(large, half - 1)
    return base + jnp.where(nn < max_exact, nn, large)


def _bias_tiles(rel_table):
    t = DSA_T
    dlt = jnp.arange(3, dtype=i32)[:, None, None] - 1
    rel = dlt * t + jnp.arange(t, dtype=i32)[None, :, None] - jnp.arange(t, dtype=i32)[None, None, :]
    onehot = (_t5_bucket(rel)[..., None] == jnp.arange(N_BUCKETS, dtype=i32)).astype(f32)
    tiles = jnp.einsum('dstb,bh->dhst', onehot, rel_table.astype(f32), precision=lax.Precision.HIGHEST)
    far = rel_table.astype(f32)[N_BUCKETS // 2 - 1]
    return (tiles - far[None, :, None, None]) * LOG2E


def _rope_table(n):
    half = A_ROPE // 2
    inv = ROPE_THETA ** (-jnp.arange(half, dtype=f32) / half)
    ang = jnp.arange(n).astype(f32)[:, None] * inv[None, :]
    cos, sin = jnp.cos(ang), jnp.sin(ang)
    return jnp.concatenate([cos, cos, sin, sin], axis=1)


def _split_hi_lo(w):
    hi = w.astype(bf16)
    lo = (w - hi.astype(f32)).astype(bf16)
    return hi, lo


def kernel(x, meta_tokens, rel_bias_table, ev_w_in, mla_q_norm, mla_kv_norm, mla_w_uq, mla_w_ukv,
           ev_w_out, od_w_in, od_w_out, ln_g, ln_b, router_w, router_b, exp_w_in, exp_b_in,
           exp_w_out, exp_b_out):
    bsz, seq, d = x.shape
    depth = ln_g.shape[0]
    alpha = (2 * depth) ** 0.25
    n_tot = seq + N_META
    n = -(-n_tot // Q_BLOCK) * Q_BLOCK
    m = bsz * n
    assert m % TOK_BLK == 0 and d % (2 * LANE) == 0
    h = jnp.concatenate([jnp.broadcast_to(meta_tokens[None].astype(x.dtype), (bsz, N_META, d)), x,
                         jnp.zeros((bsz, n - n_tot, d), x.dtype)], axis=1).reshape(m, d)
    hb = h.astype(bf16)
    k_sel = min(TOPK_MAX, seq // 4)
    tab = _rope_table(n)
    tiles = _bias_tiles(rel_bias_table)
    for i in range(depth):
        j = i // 2
        if i % 2 == 0:
            w_in2, wq2, wkv2 = _prep_even(ev_w_in[j], mla_w_uq[j], mla_w_ukv[j])
            qa, ka, va, sbq, sbk, sbv = _even_proj(hb, w_in2, mla_q_norm[j][None], mla_kv_norm[j][None],
                                                   wq2, wkv2, tab, n)
            out_a = _mla_attention(qa, ka, va, bsz, n)
            out_b = _sb_attention(sbq, sbk, sbv, bsz, n)
            w_o = ev_w_out[j].astype(bf16)
            parts, weights = [out_a, out_b], [w_o[:HA * A_V], w_o[HA * A_V:]]
        else:
            w_main, w_v_t, w_w_t = _prep_odd(od_w_in[j])
            cq, ck, iq, ik, vt, wt = _odd_proj(hb, w_main, w_v_t, w_w_t, n)
            out_c = _dsa_attention(cq, ck, iq, ik, vt, wt, tiles, bsz, n, k_sel)
            parts, weights = [out_c], [od_w_out[j].astype(bf16)]
        rw = jnp.pad(router_w[i], ((0, 0), (0, LANE - N_EXPERTS)))
        rwh, rwl = _split_hi_lo(rw)
        rb = jnp.pad(router_b[i], (0, LANE - N_EXPERTS))[None]
        h1, hp, gate, idx, cnt = _mix_out(parts, weights, h, ln_g[i, 0][None], ln_b[i, 0][None], rwh, rwl, rb, alpha)
        h, hb = _moe(hp, h1, gate, idx, cnt, exp_w_in, exp_b_in[i][:, None, :],
                     exp_w_out, exp_b_out[i][:, None, :], i,
                     ln_g[i, 1][None], ln_b[i, 1][None], alpha)
    return h.reshape(bsz, n, d)[:, N_META:N_META + seq]
```

```python
import functools
import math

import numpy as np
import jax
import jax.numpy as jnp
from jax import lax
from jax.experimental import pallas as pl
from jax.experimental.pallas import tpu as pltpu

f32 = jnp.float32
bf16 = jnp.bfloat16
i32 = jnp.int32

CHUNK = 64
N_META = 16
Q_BLOCK = 128
HA, A_NOPE, A_ROPE, A_V = 4, 128, 64, 128
Q_LORA, KV_LORA = 512, 256
ROPE_THETA = 10000.0
HB, B_DH = 4, 128
HC, C_DH = 8, 128
IDX_H, IDX_DH = 8, 64
TOPK_MAX = 256
N_BUCKETS, MAX_DIST = 32, 128
N_EXPERTS, TOP_K, D_FF = 32, 4, 512
SWIGLU_LIMIT, SWIGLU_ALPHA = 7.0, 1.702
LN_EPS, RMS_EPS = 1e-5, 1e-6

LANE = 128
SUBLANE = 8
VMEM_LIMIT = 56 * 1024 * 1024

NEG = -1e30
INT_MIN = -(2 ** 31)
CID_SHIFT = CHUNK - N_META
CHUNK_LOG2 = 6
MOE_ROWS = 256
TOK_BLK = 512
SB_EXIT = -110.0


def _cparams(sem):
    return pltpu.CompilerParams(dimension_semantics=sem, vmem_limit_bytes=VMEM_LIMIT)


def _pick(n, cands):
    for c in cands:
        if n % c == 0:
            return c
    raise ValueError(f"no block size for {n}")


def _nt(a, b):
    return lax.dot_general(a, b, (((1,), (1,)), ((), ())), preferred_element_type=f32)


def _rms(x, g):
    return x * lax.rsqrt(jnp.mean(x * x, axis=-1, keepdims=True) + RMS_EPS) * g


EV_N = Q_LORA + KV_LORA + 3 * HB * B_DH + 2 * A_ROPE


def _rope_pair(block, tab, keep_low_only):
    prod = block * tab
    r = prod + pltpu.roll(prod, A_ROPE, 1)
    if keep_low_only:
        lane = lax.broadcasted_iota(i32, r.shape, 1)
        r = jnp.where(lane < A_ROPE, r, 0.0)
    return r


def _even_proj_kernel(x_ref, w_ref, qn_ref, kvn_ref, wuq_ref, wukv_ref, tab_ref,
                      qa_ref, ka_ref, va_ref, sbq_ref, sbk_ref, sbv_ref):
    acc = jnp.dot(x_ref[...], w_ref[...], preferred_element_type=f32)
    o = Q_LORA + KV_LORA
    hb = HB * B_DH
    sbq_ref[...] = (acc[:, o:o + hb] * (B_DH ** -0.5)).astype(bf16)
    sbk_ref[...] = acc[:, o + hb:o + 2 * hb].astype(bf16)
    sbv_ref[...] = acc[:, o + 2 * hb:o + 3 * hb].astype(bf16)
    tab = tab_ref[...]
    k_rope = _rope_pair(acc[:, o + 3 * hb:o + 3 * hb + 2 * A_ROPE], tab, False).astype(bf16)
    qn = _rms(acc[:, :Q_LORA], qn_ref[...]).astype(bf16)
    kvn = _rms(acc[:, Q_LORA:o], kvn_ref[...]).astype(bf16)
    qraw = jnp.dot(qn, wuq_ref[...], preferred_element_type=f32)
    kva = jnp.dot(kvn, wukv_ref[...], preferred_element_type=f32)
    scale = (A_NOPE + A_ROPE) ** -0.5 * LOG2E
    for h in range(HA):
        b0 = h * 2 * LANE
        qa_ref[:, b0:b0 + LANE] = (qraw[:, b0:b0 + LANE] * scale).astype(bf16)
        qr = _rope_pair(qraw[:, b0 + LANE:b0 + 2 * LANE], tab, True)
        qa_ref[:, b0 + LANE:b0 + 2 * LANE] = (qr * scale).astype(bf16)
        ka_ref[:, b0:b0 + LANE] = kva[:, h * A_NOPE:(h + 1) * A_NOPE].astype(bf16)
        ka_ref[:, b0 + LANE:b0 + 2 * LANE] = k_rope
    va_ref[...] = kva[:, HA * A_NOPE:].astype(bf16)


def _even_proj(hb, w_in, q_norm, kv_norm, w_uq, w_ukv, tab, n):
    m, d = hb.shape
    tm = _pick(n, (640, 512, 256, 128))
    nb = n // tm
    const = lambda i: (0, 0)
    row = lambda i: (i, 0)
    outs = [jax.ShapeDtypeStruct((m, c), bf16) for c in
            (HA * 2 * LANE, HA * 2 * LANE, HA * A_V, HB * B_DH, HB * B_DH, HB * B_DH)]
    return pl.pallas_call(
        _even_proj_kernel,
        out_shape=outs,
        grid=(m // tm,),
        in_specs=[pl.BlockSpec((tm, d), row),
                  pl.BlockSpec(w_in.shape, const),
                  pl.BlockSpec(q_norm.shape, const),
                  pl.BlockSpec(kv_norm.shape, const),
                  pl.BlockSpec(w_uq.shape, const),
                  pl.BlockSpec(w_ukv.shape, const),
                  pl.BlockSpec((tm, LANE), lambda i: (i % nb, 0))],
        out_specs=[pl.BlockSpec((tm, s.shape[1]), row) for s in outs],
        compiler_params=_cparams(("parallel",)),
        name="even_proj",
    )(hb, w_in, q_norm, kv_norm, w_uq, w_ukv, tab)


MLA_HG = 2


def _mla_kernel(q_ref, k_ref, v_ref, o_ref, m_ref, l_ref, acc_ref, sa_ref, sb_ref, *, t, nblk):
    qi = pl.program_id(2)
    m_ref[...] = jnp.full(m_ref.shape, NEG, f32)
    l_ref[...] = jnp.zeros(l_ref.shape, f32)
    acc_ref[...] = jnp.zeros(acc_ref.shape, f32)
    qs = [q_ref[:, g * 2 * LANE:(g + 1) * 2 * LANE] for g in range(MLA_HG)]

    def scores(j, dst):
        off = pl.multiple_of(j * t, t)
        for g in range(MLA_HG):
            dst[g] = _nt(qs[g], k_ref[pl.ds(off, t), g * 2 * LANE:(g + 1) * 2 * LANE])

    def step(off, tk, masked, src=None):
        if masked:
            qpos = qi * t + lax.broadcasted_iota(i32, (t, tk), 0)
            kpos = off + lax.broadcasted_iota(i32, (t, tk), 1)
            vis = ((kpos + CID_SHIFT) >> CHUNK_LOG2) <= ((qpos + CID_SHIFT) >> CHUNK_LOG2)
        for g in range(MLA_HG):
            if src is None:
                s = _nt(qs[g], k_ref[pl.ds(off, tk), g * 2 * LANE:(g + 1) * 2 * LANE])
            else:
                s = src[g]
            if masked:
                s = jnp.where(vis, s, NEG)
            m_old = m_ref[g]
            m_new = jnp.maximum(m_old, jnp.max(s, axis=1, keepdims=True))
            alpha = jnp.exp2(m_old - m_new)
            p = jnp.exp2(s - m_new)
            l_ref[g] = alpha * l_ref[g] + jnp.sum(p, axis=1, keepdims=True)
            acc_ref[g] = alpha * acc_ref[g] + jnp.dot(p.astype(bf16), v_ref[pl.ds(off, tk), g * A_V:(g + 1) * A_V],
                                                      preferred_element_type=f32)
            m_ref[g] = m_new

    npair = qi >> 1

    @pl.when(npair > 0)
    def _():
        scores(0, sa_ref)

    def pair_body(i, c):
        j = 2 * i
        scores(j + 1, sb_ref)
        step(pl.multiple_of(j * t, t), t, False, sa_ref)
        scores(2 * jnp.minimum(i + 1, npair - 1), sa_ref)
        step(pl.multiple_of((j + 1) * t, t), t, False, sb_ref)
        return c

    lax.fori_loop(0, npair, pair_body, 0)

    @pl.when((qi & 1) == 1)
    def _():
        step(pl.multiple_of((qi - 1) * t, t), t, False)

    step(pl.multiple_of(qi * t, t), t, True)

    @pl.when(qi + 1 < nblk)
    def _():
        step(pl.multiple_of((qi + 1) * t, t), Q_BLOCK, True)

    for g in range(MLA_HG):
        o_ref[:, g * A_V:(g + 1) * A_V] = (acc_ref[g] / l_ref[g]).astype(bf16)


def _mla_attention(qa, ka, va, bsz, n):
    m = qa.shape[0]
    t = _pick(n, (640, 512, 256, 128))
    nblk = n // t
    hg = MLA_HG
    return pl.pallas_call(
        functools.partial(_mla_kernel, t=t, nblk=nblk),
        out_shape=jax.ShapeDtypeStruct((m, HA * A_V), bf16),
        grid=(bsz, HA // hg, nblk),
        in_specs=[pl.BlockSpec((t, hg * 2 * LANE), lambda b, h, i: (b * nblk + i, h)),
                  pl.BlockSpec((n, hg * 2 * LANE), lambda b, h, i: (b, h)),
                  pl.BlockSpec((n, hg * A_V), lambda b, h, i: (b, h))],
        out_specs=pl.BlockSpec((t, hg * A_V), lambda b, h, i: (b * nblk + i, h)),
        scratch_shapes=[pltpu.VMEM((hg, t, 1), f32), pltpu.VMEM((hg, t, 1), f32), pltpu.VMEM((hg, t, A_V), f32),
                        pltpu.VMEM((hg, t, t), f32), pltpu.VMEM((hg, t, t), f32)],
        compiler_params=_cparams(("parallel", "parallel", "parallel")),
        name="mla_attention",
    )(qa, ka, va)


SB_T = 128


def _sb_kernel(q_ref, k_ref, v_ref, u_ref, o_ref):
    t = SB_T
    qi = pl.program_id(1)
    u = u_ref[...]
    qs = [q_ref[:, g * B_DH:(g + 1) * B_DH] for g in range(HB)]

    def block(j, cs, accs, masked):
        off = pl.multiple_of(j * t, t)
        if masked:
            row = lax.broadcasted_iota(i32, (t, t), 0)
            col = lax.broadcasted_iota(i32, (t, t), 1)
            causal = col < row
        new_c, new_acc = [], []
        for g in range(HB):
            z = _nt(qs[g], k_ref[pl.ds(off, t), g * B_DH:(g + 1) * B_DH])
            lk = -(jnp.maximum(z, 0.0) + jnp.log(1.0 + jnp.exp(-jnp.abs(z))))
            if masked:
                lk = jnp.where(causal, lk, 0.0)
            hi = lk.astype(bf16)
            lo = (lk - hi.astype(f32)).astype(bf16)
            cum = jnp.dot(hi, u, preferred_element_type=f32) + jnp.dot(lo, u, preferred_element_type=f32)
            a = jnp.exp(z + cs[g] + cum)
            if masked:
                a = jnp.where(causal, a, 0.0)
            new_acc.append(accs[g] + jnp.dot(a.astype(bf16), v_ref[pl.ds(off, t), g * B_DH:(g + 1) * B_DH],
                                             preferred_element_type=f32))
            new_c.append(cs[g] + cum[:, 0:1])
        return tuple(new_c), tuple(new_acc)

    cs, accs = block(qi, (jnp.zeros((t, 1), f32),) * HB, (jnp.zeros((t, B_DH), f32),) * HB, True)

    def cond(st):
        j, cs, _ = st
        cmax = cs[0]
        for c in cs[1:]:
            cmax = jnp.maximum(cmax, c)
        return jnp.logical_and(j >= 0, jnp.max(cmax) > SB_EXIT)

    def body(st):
        j, cs, accs = st
        cs, accs = block(j, cs, accs, False)
        return j - 1, cs, accs

    _, _, accs = lax.while_loop(cond, body, (qi - 1, cs, accs))
    for g in range(HB):
        o_ref[:, g * B_DH:(g + 1) * B_DH] = accs[g].astype(bf16)


def _sb_attention(sbq, sbk, sbv, bsz, n):
    m = sbq.shape[0]
    t = SB_T
    nblk = n // t
    u = (np.arange(t)[:, None] >= np.arange(t)[None, :]).astype(np.float32)
    u = jnp.asarray(u, bf16)
    w = HB * B_DH
    return pl.pallas_call(
        _sb_kernel,
        out_shape=jax.ShapeDtypeStruct((m, w), bf16),
        grid=(bsz, nblk),
        in_specs=[pl.BlockSpec((t, w), lambda b, i: (b * nblk + i, 0)),
                  pl.BlockSpec((n, w), lambda b, i: (b, 0)),
                  pl.BlockSpec((n, w), lambda b, i: (b, 0)),
                  pl.BlockSpec((t, t), lambda b, i: (0, 0))],
        out_specs=pl.BlockSpec((t, w), lambda b, i: (b * nblk + i, 0)),
        compiler_params=_cparams(("parallel", "parallel")),
        name="stick_breaking_attention",
    )(sbq, sbk, sbv, u)


def _pack_bf16_pairs(x):
    half = x.shape[1] // 2
    xb = x.astype(bf16).astype(f32)
    lo = lax.bitcast_convert_type(xb[:, :half], i32)
    hi = lax.bitcast_convert_type(xb[:, half:], i32)
    return ((lo >> 16) & 0xFFFF) | (hi & (-65536))


def _unpack_bf16_pairs(w):
    lo = lax.bitcast_convert_type(w << 16, f32).astype(bf16)
    hi = lax.bitcast_convert_type(w & (-65536), f32).astype(bf16)
    return jnp.concatenate([lo, hi], axis=1)


def _layer_norm(y, g, b):
    mu = jnp.mean(y, axis=-1, keepdims=True)
    yc = y - mu
    var = jnp.mean(yc * yc, axis=-1, keepdims=True)
    return yc * lax.rsqrt(var + LN_EPS) * g + b


def _mix_out_kernel(*refs, nparts, alpha):
    a_refs = refs[:nparts]
    w_refs = refs[nparts:2 * nparts]
    h_ref, g_ref, b_ref, rwh_ref, rwl_ref, rb_ref = refs[2 * nparts:2 * nparts + 6]
    h1_ref, hp_ref, gate_ref, idx_ref, cnt_ref = refs[2 * nparts + 6:]
    mix = jnp.dot(a_refs[0][...], w_refs[0][...], preferred_element_type=f32)
    for a_ref, w_ref in zip(a_refs[1:], w_refs[1:]):
        mix = mix + jnp.dot(a_ref[...], w_ref[...], preferred_element_type=f32)
    hn = _layer_norm(alpha * h_ref[...] + mix, g_ref[...], b_ref[...])
    h1_ref[...] = hn
    hp_ref[...] = _pack_bf16_pairs(hn)
    x_hi = hn.astype(bf16)
    x_lo = (hn - x_hi.astype(f32)).astype(bf16)
    rwh = rwh_ref[...]
    logits = (jnp.dot(x_hi, rwh, preferred_element_type=f32) + jnp.dot(x_lo, rwh, preferred_element_type=f32)
              + jnp.dot(x_hi, rwl_ref[...], preferred_element_type=f32) + rb_ref[...])
    lane_i = lax.broadcasted_iota(i32, logits.shape, 1)
    lane = lane_i.astype(f32)
    cur = jnp.where(lane_i < N_EXPERTS, logits, -jnp.inf)
    vals, idxs = [], []
    for _ in range(TOP_K):
        mx = jnp.max(cur, axis=1, keepdims=True)
        ix = jnp.min(jnp.where(cur == mx, lane, float(LANE)), axis=1, keepdims=True)
        vals.append(mx)
        idxs.append(ix)
        cur = jnp.where(lane == ix, -jnp.inf, cur)
    es = [jnp.exp(v - vals[0]) for v in vals]
    den = es[0]
    for e in es[1:]:
        den = den + e
    gate = jnp.zeros(logits.shape, f32)
    idx = jnp.zeros(logits.shape, f32)
    picked = jnp.zeros(logits.shape, f32)
    for k in range(TOP_K):
        gate = jnp.where(lane_i == k, es[k] / den, gate)
        idx = jnp.where(lane_i == k, idxs[k], idx)
        picked = picked + jnp.where(lane == idxs[k], 1.0, 0.0)
    gate_ref[...] = gate
    idx_ref[...] = idx.astype(i32)

    @pl.when(pl.program_id(0) == 0)
    def _():
        cnt_ref[...] = jnp.zeros(cnt_ref.shape, f32)

    cnt_ref[0:1, :] = cnt_ref[0:1, :] + jnp.sum(picked, axis=0, keepdims=True)


def _mix_out(parts, weights, h, g, b, rwh, rwl, rb, alpha):
    m, d = h.shape
    tm = _pick(m, (256, 128))
    row = lambda i: (i, 0)
    const = lambda i: (0, 0)
    nparts = len(parts)
    outs = [jax.ShapeDtypeStruct((m, d), f32), jax.ShapeDtypeStruct((m, d // 2), i32),
            jax.ShapeDtypeStruct((m, LANE), f32), jax.ShapeDtypeStruct((m, LANE), i32),
            jax.ShapeDtypeStruct((SUBLANE, LANE), f32)]
    in_specs = ([pl.BlockSpec((tm, p.shape[1]), row) for p in parts]
                + [pl.BlockSpec(w.shape, const) for w in weights]
                + [pl.BlockSpec((tm, d), row), pl.BlockSpec((1, d), const), pl.BlockSpec((1, d), const),
                   pl.BlockSpec(rwh.shape, const), pl.BlockSpec(rwl.shape, const), pl.BlockSpec((1, LANE), const)])
    return pl.pallas_call(
        functools.partial(_mix_out_kernel, nparts=nparts, alpha=alpha),
        out_shape=outs,
        grid=(m // tm,),
        in_specs=in_specs,
        out_specs=([pl.BlockSpec((tm, s.shape[1]), row) for s in outs[:-1]]
                   + [pl.BlockSpec((SUBLANE, LANE), const)]),
        compiler_params=_cparams(("arbitrary",)),
        name="mix_out_ln_router",
    )(*parts, *weights, h, g, b, rwh, rwl, rb)


def _row_copy(src, dst, sem):
    return pltpu.make_async_copy(src, dst, sem)


def _dest_kernel(idx_ref, tri_ref, start_ref, dest_ref, carry_ref):
    @pl.when(pl.program_id(0) == 0)
    def _():
        carry_ref[...] = jnp.zeros(carry_ref.shape, f32)

    idx = idx_ref[...]
    lane = lax.broadcasted_iota(i32, idx.shape, 1)
    hits = [lane == idx[:, k:k + 1] for k in range(TOP_K)]
    onehot = jnp.where(hits[0], 1.0, 0.0)
    for hit in hits[1:]:
        onehot = onehot + jnp.where(hit, 1.0, 0.0)
    carry = carry_ref[0:1, :]
    base = jnp.dot(tri_ref[...], onehot.astype(bf16), preferred_element_type=f32) + carry + start_ref[...]
    dest = jnp.zeros(idx.shape, f32)
    for k in range(TOP_K):
        dest = jnp.where(lane == k, jnp.sum(jnp.where(hits[k], base, 0.0), axis=1, keepdims=True), dest)
    dest_ref[...] = dest.T[0:SUBLANE, :].astype(i32)
    carry_ref[0:1, :] = carry + jnp.sum(onehot, axis=0, keepdims=True)


def _dest_rows(idx, start):
    m = idx.shape[0]
    tm = TOK_BLK
    tri = jnp.asarray(np.tril(np.ones((tm, tm), np.float32), -1), bf16)
    return pl.pallas_call(
        _dest_kernel,
        out_shape=jax.ShapeDtypeStruct((m // tm, SUBLANE, tm), i32),
        grid=(m // tm,),
        in_specs=[pl.BlockSpec((tm, LANE), lambda i: (i, 0)),
                  pl.BlockSpec((tm, tm), lambda i: (0, 0)),
                  pl.BlockSpec((1, LANE), lambda i: (0, 0))],
        out_specs=pl.BlockSpec((None, SUBLANE, tm), lambda i: (i, 0, 0)),
        scratch_shapes=[pltpu.VMEM((SUBLANE, LANE), f32)],
        compiler_params=_cparams(("arbitrary",)),
        name="moe_dest_rows",
    )(idx, tri, start)


def _step_indices(dest_hbm, dest_smem, sem_idx, n_idx):
    i = pl.program_id(0)
    slot = i & 1

    def copy(step, s):
        dst = dest_smem.at[pl.ds(pl.multiple_of(s * n_idx, n_idx), n_idx)]
        return pltpu.make_async_copy(dest_hbm.at[step], dst, sem_idx.at[s])

    @pl.when(i == 0)
    def _():
        copy(0, 0).start()

    copy(i, slot).wait()

    @pl.when(i + 1 < pl.num_programs(0))
    def _():
        copy(i + 1, 1 - slot).start()

    return slot * n_idx


def _dispatch_kernel(lo_ref, hi_ref, dest_hbm, hp_ref, xs_out, dest_smem, zrow, sem_idx, sem_rows, sem_pad):
    i = pl.program_id(0)
    tm = hp_ref.shape[0]
    idx0 = _step_indices(dest_hbm, dest_smem, sem_idx, tm * TOP_K)

    def issue(g, c):
        base = pl.multiple_of(g * SUBLANE, SUBLANE)
        for s in range(SUBLANE):
            for k in range(TOP_K):
                d = dest_smem[idx0 + k * tm + base + s]
                _row_copy(hp_ref.at[pl.ds(base + s, 1)], xs_out.at[pl.ds(d, 1)], sem_rows).start()
        return c

    lax.fori_loop(0, tm // SUBLANE, issue, 0)
    for k in range(TOP_K):
        _row_copy(hp_ref, xs_out.at[pl.ds(0, tm)], sem_rows).wait()

    @pl.when(i == pl.num_programs(0) - 1)
    def _():
        zrow[...] = jnp.zeros(zrow.shape, i32)

        def per_expert(e, c):
            lo, hi = lo_ref[e], hi_ref[e]

            def zstart(r, c2):
                _row_copy(zrow.at[pl.ds(0, 1)], xs_out.at[pl.ds(r, 1)], sem_pad).start()
                return c2

            def zwait(r, c2):
                _row_copy(zrow.at[pl.ds(0, 1)], xs_out.at[pl.ds(0, 1)], sem_pad).wait()
                return c2

            lax.fori_loop(lo, hi, zstart, 0)
            lax.fori_loop(lo, hi, zwait, 0)
            return c

        lax.fori_loop(0, N_EXPERTS, per_expert, 0)

        rows = zrow.shape[0]
        first = hi_ref[N_EXPERTS - 1] // rows
        last = xs_out.shape[0] // rows

        def bstart(b, c2):
            _row_copy(zrow, xs_out.at[pl.ds(pl.multiple_of(b * rows, rows), rows)], sem_pad).start()
            return c2

        def bwait(b, c2):
            _row_copy(zrow, xs_out.at[pl.ds(0, rows)], sem_pad).wait()
            return c2

        lax.fori_loop(first, last, bstart, 0)
        lax.fori_loop(first, last, bwait, 0)


def _dispatch(pad_lo, pad_hi, dest2d, hp, n_rows):
    m, dh = hp.shape
    tm = TOK_BLK
    grid_spec = pltpu.PrefetchScalarGridSpec(
        num_scalar_prefetch=2,
        grid=(m // tm,),
        in_specs=[pl.BlockSpec(memory_space=pl.ANY),
                  pl.BlockSpec((tm, dh), lambda i, lo, hi: (i, 0))],
        out_specs=pl.BlockSpec(memory_space=pl.ANY),
        scratch_shapes=[pltpu.SMEM((2 * tm * TOP_K,), i32), pltpu.VMEM((MOE_ROWS, dh), i32),
                        pltpu.SemaphoreType.DMA((2,)), pltpu.SemaphoreType.DMA, pltpu.SemaphoreType.DMA],
    )
    return pl.pallas_call(
        _dispatch_kernel,
        out_shape=jax.ShapeDtypeStruct((n_rows, dh), i32),
        grid_spec=grid_spec,
        compiler_params=_cparams(("arbitrary",)),
        name="moe_dispatch",
    )(pad_lo, pad_hi, dest2d, hp)


def _expert_kernel(be_ref, nu_ref, x_ref, wi_ref, bi_ref, wo_ref, bo_ref, y_ref, wib_ref, wob_ref):
    i = pl.program_id(0)

    @pl.when(jnp.logical_or(i == 0, be_ref[i] != be_ref[jnp.maximum(i - 1, 0)]))
    def _():
        wib_ref[...] = wi_ref[...].astype(bf16)
        wob_ref[...] = wo_ref[...].astype(bf16)

    @pl.when(i >= nu_ref[0])
    def _():
        y_ref[...] = jnp.zeros(y_ref.shape, f32)

    @pl.when(i < nu_ref[0])
    def _():
        xb = _unpack_bf16_pairs(x_ref[...])
        hh = jnp.dot(xb, wib_ref[...], preferred_element_type=f32) + bi_ref[...]
        gate = jnp.minimum(hh[:, :D_FF], SWIGLU_LIMIT)
        up = jnp.clip(hh[:, D_FF:], -SWIGLU_LIMIT, SWIGLU_LIMIT)
        act = gate * (1.0 / (1.0 + jnp.exp(-SWIGLU_ALPHA * gate))) * (up + 1.0)
        y_ref[...] = jnp.dot(act.astype(bf16), wob_ref[...], preferred_element_type=f32) + bo_ref[...]


def _experts(block_e, n_used, xs, w_in, b_in, w_out, b_out, layer):
    n_rows, dh = xs.shape
    d = 2 * dh
    nb = n_rows // MOE_ROWS
    blk = lambda i, be, nu: (jnp.minimum(i, nu[0] - 1), 0)
    per_e = lambda i, be, nu: (be[i], 0, 0)
    per_le = lambda i, be, nu: (layer, be[i], 0, 0)
    grid_spec = pltpu.PrefetchScalarGridSpec(
        num_scalar_prefetch=2,
        grid=(nb,),
        in_specs=[pl.BlockSpec((MOE_ROWS, dh), blk),
                  pl.BlockSpec((None, None, d, 2 * D_FF), per_le),
                  pl.BlockSpec((None, 1, 2 * D_FF), per_e),
                  pl.BlockSpec((None, None, D_FF, d), per_le),
                  pl.BlockSpec((None, 1, d), per_e)],
        out_specs=pl.BlockSpec((MOE_ROWS, d), lambda i, be, nu: (i, 0)),
        scratch_shapes=[pltpu.VMEM((d, 2 * D_FF), bf16), pltpu.VMEM((D_FF, d), bf16)],
    )
    return pl.pallas_call(
        _expert_kernel,
        out_shape=jax.ShapeDtypeStruct((n_rows, d), f32),
        grid_spec=grid_spec,
        compiler_params=_cparams(("arbitrary",)),
        name="moe_experts",
    )(block_e, n_used, xs, w_in, b_in, w_out, b_out)


def _combine_kernel(dest_hbm, y_hbm, gate_ref, h_ref, g_ref, b_ref, h2_ref, h2b_ref,
                    dest_smem, ybuf, sem_idx, sem_rows, *, alpha):
    tm = h_ref.shape[0]
    idx0 = _step_indices(dest_hbm, dest_smem, sem_idx, tm * TOP_K)

    def issue(g, c):
        base = pl.multiple_of(g * SUBLANE, SUBLANE)
        for s in range(SUBLANE):
            for k in range(TOP_K):
                d = dest_smem[idx0 + k * tm + base + s]
                _row_copy(y_hbm.at[pl.ds(d, 1)], ybuf.at[k, pl.ds(base + s, 1)], sem_rows).start()
        return c

    lax.fori_loop(0, tm // SUBLANE, issue, 0)
    for k in range(TOP_K):
        _row_copy(y_hbm.at[pl.ds(0, tm)], ybuf.at[k], sem_rows).wait()
    gate = gate_ref[...]
    f = gate[:, 0:1] * ybuf[0]
    for k in range(1, TOP_K):
        f = f + gate[:, k:k + 1] * ybuf[k]
    hn = _layer_norm(alpha * h_ref[...] + f, g_ref[...], b_ref[...])
    h2_ref[...] = hn
    h2b_ref[...] = hn.astype(bf16)


def _combine(dest2d, y, gate, h, g, b, alpha):
    m, d = h.shape
    tm = TOK_BLK
    row = lambda i: (i, 0)
    const = lambda i: (0, 0)
    return pl.pallas_call(
        functools.partial(_combine_kernel, alpha=alpha),
        out_shape=[jax.ShapeDtypeStruct((m, d), f32), jax.ShapeDtypeStruct((m, d), bf16)],
        grid=(m // tm,),
        in_specs=[pl.BlockSpec(memory_space=pl.ANY),
                  pl.BlockSpec(memory_space=pl.ANY),
                  pl.BlockSpec((tm, LANE), row),
                  pl.BlockSpec((tm, d), row),
                  pl.BlockSpec((1, d), const),
                  pl.BlockSpec((1, d), const)],
        out_specs=[pl.BlockSpec((tm, d), row), pl.BlockSpec((tm, d), row)],
        scratch_shapes=[pltpu.SMEM((2 * tm * TOP_K,), i32), pltpu.VMEM((TOP_K, tm, d), f32),
                        pltpu.SemaphoreType.DMA((2,)), pltpu.SemaphoreType.DMA],
        compiler_params=_cparams(("arbitrary",)),
        name="moe_combine_ln",
    )(dest2d, y, gate, h, g, b)


def _route(idx, cnt, m):
    counts = cnt[0, :N_EXPERTS].astype(i32)
    padded = (counts + MOE_ROWS - 1) // MOE_ROWS * MOE_ROWS
    padded_end = jnp.cumsum(padded)
    start_padded = padded_end - padded
    n_blocks = -(-(m * TOP_K) // MOE_ROWS) + N_EXPERTS
    first_row = jnp.arange(n_blocks, dtype=i32) * MOE_ROWS
    block_e = jnp.minimum(jnp.sum((first_row[:, None] >= padded_end[None, :]).astype(i32), axis=1), N_EXPERTS - 1)
    n_used = (padded_end[-1:] // MOE_ROWS).astype(i32)
    start = jnp.pad(start_padded.astype(f32), (0, LANE - N_EXPERTS))[None]
    dest = _dest_rows(idx, start)
    dest2d = dest[:, :TOP_K, :].reshape(m // TOK_BLK, TOP_K * TOK_BLK)
    return dest2d, block_e, n_used, start_padded + counts, padded_end, n_blocks * MOE_ROWS


def _moe(hp, h1, gate, idx, cnt, w_in, b_in, w_out, b_out, layer, g, b, alpha):
    m = h1.shape[0]
    dest2d, block_e, n_used, pad_lo, pad_hi, n_rows = _route(idx, cnt, m)
    xs = _dispatch(pad_lo, pad_hi, dest2d, hp, n_rows)
    y = _experts(block_e, n_used, xs, w_in, b_in, w_out, b_out, layer)
    return _combine(dest2d, y, gate, h1, g, b, alpha)


OD_Q = HC * C_DH
VT_ROWS = C_DH + 2 * SUBLANE
LOG2E = math.log2(math.e)


def _odd_proj_kernel(x_ref, w_ref, wv_ref, ww_ref, cq_ref, ck_ref, iq_ref, ik_ref, vt_ref, wt_ref):
    x = x_ref[...]
    acc = jnp.dot(x, w_ref[...], preferred_element_type=f32)
    cq_ref[...] = (acc[:, :OD_Q] * (C_DH ** -0.5 * LOG2E)).astype(bf16)
    ck_ref[...] = acc[:, OD_Q:OD_Q + C_DH].astype(bf16)
    o = OD_Q + C_DH
    iq_ref[...] = (acc[:, o:o + IDX_H * LANE] * (IDX_DH ** -0.5)).astype(bf16)
    ik_ref[...] = acc[:, o + IDX_H * LANE:].astype(bf16)
    vt = _nt(wv_ref[...], x).astype(bf16)
    extra = (VT_ROWS - C_DH, vt.shape[1])
    ones_row = jnp.where(lax.broadcasted_iota(i32, extra, 0) == 0, 1.0, 0.0).astype(bf16)
    vt = jnp.concatenate([vt, ones_row], axis=0)
    for c in range(vt_ref.shape[0]):
        vt_ref[c] = vt[:, c * LANE:(c + 1) * LANE]
    wt_ref[...] = _nt(ww_ref[...], x)[:IDX_H] * (IDX_H ** -0.5)


def _odd_proj(hb, w_main, w_v_t, w_w_t, n):
    m, d = hb.shape
    tm = _pick(n, (640, 512, 256, 128))
    row = lambda i: (i, 0)
    const = lambda i: (0, 0)
    outs = [jax.ShapeDtypeStruct((m, OD_Q), bf16), jax.ShapeDtypeStruct((m, C_DH), bf16),
            jax.ShapeDtypeStruct((m, IDX_H * LANE), bf16), jax.ShapeDtypeStruct((m, LANE), bf16),
            jax.ShapeDtypeStruct((m // LANE, VT_ROWS, LANE), bf16), jax.ShapeDtypeStruct((IDX_H, m), f32)]
    return pl.pallas_call(
        _odd_proj_kernel,
        out_shape=outs,
        grid=(m // tm,),
        in_specs=[pl.BlockSpec((tm, d), row), pl.BlockSpec(w_main.shape, const),
                  pl.BlockSpec(w_v_t.shape, const), pl.BlockSpec(w_w_t.shape, const)],
        out_specs=[pl.BlockSpec((tm, OD_Q), row), pl.BlockSpec((tm, C_DH), row),
                   pl.BlockSpec((tm, IDX_H * LANE), row), pl.BlockSpec((tm, LANE), row),
                   pl.BlockSpec((tm // LANE, VT_ROWS, LANE), lambda i: (i, 0, 0)),
                   pl.BlockSpec((IDX_H, tm), lambda i: (0, i))],
        compiler_params=_cparams(("parallel",)),
        name="odd_proj",
    )(hb, w_main, w_v_t, w_w_t)


DSA_T = 128
IDX_BITS = 14


DSA_SPAN = 4
NO_CUT = 2 ** 30


def _for_blocks(lo, hi, fn):
    def body(i, c):
        fn(lo + 2 * i, 2)
        return c

    lax.fori_loop(0, (hi - lo) >> 1, body, 0)

    @pl.when(((hi - lo) & 1) == 1)
    def _():
        fn(hi - 1, 1)


def _pipelined_blocks(hi, qk, use, sta, stb):
    nquad = hi >> 2

    @pl.when(nquad > 0)
    def _():
        sta[...] = qk(0, 2)

    def body(i, c):
        j = 4 * i
        stb[...] = qk(j + 2, 2)
        use(j, 2, sta)
        sta[...] = qk(4 * jnp.minimum(i + 1, nquad - 1), 2)
        use(j + 2, 2, stb)
        return c

    lax.fori_loop(0, nquad, body, 0)
    _for_blocks(4 * nquad, hi, lambda j, nb: use(j, nb, None))


def _dsa_kernel(cq_ref, iq_ref, wt_ref, ik_ref, ck_ref, vt_ref, tile_ref, o_ref,
                ks_ref, lim_ref, acc_ref, m_ref, sta_ref, stb_ref, *, nblk, k_sel):
    t = DSA_T
    qi = pl.program_id(1)
    nkb = jnp.minimum(qi + 2, nblk)
    kf = float(k_sel)

    def key_pos(off, rows):
        return off + lax.broadcasted_iota(i32, (rows, t), 0)

    iq = iq_ref[...]
    iq_stack = jnp.concatenate([iq[:, h * LANE:(h + 1) * LANE] for h in range(IDX_H)], axis=0)
    wt = wt_ref[...]

    def score_qk(j, nb):
        return _nt(ik_ref[pl.ds(pl.multiple_of(j * t, t), nb * t), :], iq_stack)

    def score_span(j, nb, masked, st=None):
        rows = nb * t
        off = pl.multiple_of(j * t, t)
        if st is None:
            st = score_qk(j, nb)
        sc = jnp.maximum(st[:, :t], 0.0) * wt[0:1, :]
        for h in range(1, IDX_H):
            sc = sc + jnp.maximum(st[:, h * t:(h + 1) * t], 0.0) * wt[h:h + 1, :]
        sc = jnp.where(sc == 0.0, 0.0, sc)
        bits = lax.bitcast_convert_type(sc, i32)
        key = jnp.where(bits < 0, bits ^ 0x7FFFFFFF, bits)
        if masked:
            tpos = qi * t + lax.broadcasted_iota(i32, (rows, t), 1)
            adm = ((key_pos(off, rows) + CID_SHIFT) >> CHUNK_LOG2) <= ((tpos + CID_SHIFT) >> CHUNK_LOG2)
            key = jnp.where(adm, key, INT_MIN)
        ks_ref[pl.ds(off, rows), :] = key

    _pipelined_blocks(qi, score_qk, lambda j, nb, st: score_span(j, nb, False, st), sta_ref, stb_ref)
    _for_blocks(qi, nkb, lambda j, nb: score_span(j, nb, True))
    ks_ref[pl.ds(pl.multiple_of(nkb * t, t), (DSA_SPAN - 1) * t), :] = jnp.full(((DSA_SPAN - 1) * t, t), INT_MIN, i32)

    span_rows = DSA_SPAN * t
    nspan = (nkb + DSA_SPAN - 1) // DSA_SPAN

    def count(pred_fn):
        lanes = 8 * SUBLANE

        def body(i, cnt):
            off = pl.multiple_of(i * span_rows, span_rows)
            hit = jnp.where(pred_fn(ks_ref[pl.ds(off, span_rows), :], off), 1.0, 0.0)
            return cnt + jnp.sum(hit.reshape(span_rows // lanes, lanes, t), axis=0)
        cnt = lax.fori_loop(0, nspan, body, jnp.zeros((lanes, t), f32))
        return jnp.sum(cnt, axis=0, keepdims=True)

    n_adm = count(lambda kk, off: kk > INT_MIN)

    def search_cond(st):
        b, _, cnt = st
        return jnp.logical_and(b < 32, jnp.max(cnt) > kf)

    def search_body(st):
        b, thr, cnt = st
        for _ in range(2):
            cand = thr + lax.shift_left(jnp.int32(1), 31 - b)
            c = count(lambda kk, off: kk >= cand)
            ok = c >= kf
            b, thr, cnt = b + 1, jnp.where(ok, cand, thr), jnp.where(ok, c, cnt)
        return b, thr, cnt

    _, thr, cnt = lax.while_loop(search_cond, search_body,
                                 (jnp.int32(0), jnp.full((1, t), INT_MIN, i32), n_adm))
    thr = jnp.maximum(thr, INT_MIN + 1)

    lim_ref[...] = jnp.full(lim_ref.shape, NO_CUT, i32)

    @pl.when(jnp.max(cnt) > kf)
    def _():
        need = kf - count(lambda kk, off: kk > thr)

        def idx_step(b, lim):
            cand = lim + lax.shift_left(jnp.int32(1), IDX_BITS - 1 - b)
            c = count(lambda kk, off: jnp.logical_and(kk == thr, key_pos(off, span_rows) < cand))
            return jnp.where(c <= need, cand, lim)

        lim = lax.fori_loop(0, IDX_BITS, idx_step, jnp.zeros((1, t), i32))
        lim_ref[0:1, :] = jnp.where(cnt > kf, lim, NO_CUT)

    lim = lim_ref[0:1, :]

    cq = cq_ref[...]
    q_stack = jnp.concatenate([cq[:, h * LANE:(h + 1) * LANE] for h in range(HC)], axis=0)
    m_ref[...] = jnp.full(m_ref.shape, NEG, f32)
    acc_ref[...] = jnp.zeros(acc_ref.shape, f32)

    def attn_qk(j, nb):
        return _nt(ck_ref[pl.ds(pl.multiple_of(j * t, t), nb * t), :], q_stack)

    def attn_span(j, nb, near, st=None):
        rows = nb * t
        off = pl.multiple_of(j * t, t)
        kk = ks_ref[pl.ds(off, rows), :]
        sel = kk >= jnp.where(key_pos(off, rows) < lim, thr, thr + 1)
        if st is None:
            st = attn_qk(j, nb)
        m_old = m_ref[...]
        ps, m_news = [], []
        for h in range(HC):
            s = st[:, h * t:(h + 1) * t]
            if near:
                ti = j - qi + 1
                s = s + (tile_ref[ti, h] if nb == 1 else
                         jnp.concatenate([tile_ref[ti, h], tile_ref[ti + 1, h]], axis=0))
            s = jnp.where(sel, s, NEG)
            m_new = jnp.maximum(m_old[:, h * t:(h + 1) * t], jnp.max(s, axis=0, keepdims=True))
            ps.append(jnp.exp2(s - m_new))
            m_news.append(m_new)
        p = jnp.concatenate(ps, axis=1)
        m_new = jnp.concatenate(m_news, axis=1)
        alpha = jnp.exp2(m_old - m_new)
        vt = vt_ref[j] if nb == 1 else jnp.concatenate([vt_ref[j], vt_ref[j + 1]], axis=1)
        acc_ref[...] = alpha * acc_ref[...] + jnp.dot(vt, p.astype(bf16), preferred_element_type=f32)
        m_ref[...] = m_new

    def near_body(j, c):
        attn_span(j, 1, True)
        return c

    n_far = jnp.maximum(qi - 1, 0)
    _pipelined_blocks(n_far, attn_qk, lambda j, nb, st: attn_span(j, nb, False, st), sta_ref, stb_ref)
    if nblk >= 2:
        attn_span(n_far, 2, True)

        @pl.when(n_far + 2 < nkb)
        def _():
            attn_span(n_far + 2, 1, True)
    else:
        lax.fori_loop(n_far, nkb, near_body, 0)
    out_t = acc_ref[0:C_DH, :] / acc_ref[C_DH:C_DH + 1, :]
    for h in range(HC):
        o_ref[:, h * C_DH:(h + 1) * C_DH] = out_t[:, h * t:(h + 1) * t].T.astype(bf16)


def _dsa_attention(cq, ck, iq, ik, vt, wt, tiles, bsz, n, k_sel):
    m = cq.shape[0]
    t = DSA_T
    nblk = n // t
    vt4 = vt.reshape(bsz, nblk, VT_ROWS, LANE)
    return pl.pallas_call(
        functools.partial(_dsa_kernel, nblk=nblk, k_sel=k_sel),
        out_shape=jax.ShapeDtypeStruct((m, HC * C_DH), bf16),
        grid=(bsz, nblk),
        in_specs=[pl.BlockSpec((t, HC * C_DH), lambda b, i: (b * nblk + i, 0)),
                  pl.BlockSpec((t, IDX_H * LANE), lambda b, i: (b * nblk + i, 0)),
                  pl.BlockSpec((IDX_H, t), lambda b, i: (0, b * nblk + i)),
                  pl.BlockSpec((n, LANE), lambda b, i: (b, 0)),
                  pl.BlockSpec((n, C_DH), lambda b, i: (b, 0)),
                  pl.BlockSpec((None, nblk, VT_ROWS, LANE), lambda b, i: (b, 0, 0, 0)),
                  pl.BlockSpec(tiles.shape, lambda b, i: (0, 0, 0, 0))],
        out_specs=pl.BlockSpec((t, HC * C_DH), lambda b, i: (b * nblk + i, 0)),
        scratch_shapes=[pltpu.VMEM((n + (DSA_SPAN - 1) * t, t), i32), pltpu.VMEM((SUBLANE, t), i32),
                        pltpu.VMEM((VT_ROWS, HC * t), f32), pltpu.VMEM((1, HC * t), f32),
                        pltpu.VMEM((2 * t, HC * t), f32), pltpu.VMEM((2 * t, HC * t), f32)],
        compiler_params=_cparams(("parallel", "parallel")),
        name="dsa_attention",
    )(cq, iq, wt, ik, ck, vt4, tiles)


def _rot_half_cols(w):
    half = w.shape[-1] // 2
    return jnp.concatenate([-w[..., half:], w[..., :half]], axis=-1)


def _prep_even(w_in, w_uq, w_ukv):
    o0 = Q_LORA + KV_LORA
    kr = w_in[:, o0:o0 + A_ROPE]
    w_in2 = jnp.concatenate([w_in[:, :o0], w_in[:, o0 + A_ROPE:], kr, _rot_half_cols(kr)], axis=1).astype(bf16)
    wq = w_uq.reshape(Q_LORA, HA, A_NOPE + A_ROPE)
    wq_r = wq[:, :, A_NOPE:]
    wq2 = jnp.concatenate([wq[:, :, :A_NOPE], wq_r, _rot_half_cols(wq_r)], axis=-1).reshape(Q_LORA, HA * 2 * LANE)
    wkv = w_ukv.reshape(KV_LORA, HA, A_NOPE + A_V)
    wkv2 = jnp.concatenate([wkv[:, :, :A_NOPE].reshape(KV_LORA, HA * A_NOPE),
                            wkv[:, :, A_NOPE:].reshape(KV_LORA, HA * A_V)], axis=1)
    return w_in2, wq2.astype(bf16), wkv2.astype(bf16)


def _prep_odd(w_in):
    d = w_in.shape[0]
    o = OD_Q
    w_q, w_k, w_v = w_in[:, :o], w_in[:, o:o + C_DH], w_in[:, o + C_DH:o + 2 * C_DH]
    o2 = o + 2 * C_DH
    w_iq = w_in[:, o2:o2 + IDX_H * IDX_DH].reshape(d, IDX_H, IDX_DH)
    w_iq = jnp.pad(w_iq, ((0, 0), (0, 0), (0, LANE - IDX_DH))).reshape(d, IDX_H * LANE)
    o3 = o2 + IDX_H * IDX_DH
    w_ik = jnp.pad(w_in[:, o3:o3 + IDX_DH], ((0, 0), (0, LANE - IDX_DH)))
    w_iw = w_in[:, o3 + IDX_DH:o3 + IDX_DH + IDX_H]
    w_main = jnp.concatenate([w_q, w_k, w_iq, w_ik], axis=1).astype(bf16)
    w_v_t = w_v.T.astype(bf16)
    w_w_t = jnp.pad(w_iw.T, ((0, 2 * SUBLANE - IDX_H), (0, 0))).astype(bf16)
    return w_main, w_v_t, w_w_t


def _t5_bucket(rel):
    half = N_BUCKETS // 2
    max_exact = half // 2
    base = jnp.where(rel > 0, half, 0)
    nn = jnp.abs(rel)
    nf = jnp.maximum(nn, 1).astype(f32)
    large = max_exact + (jnp.log(nf / max_exact) / math.log(MAX_DIST / max_exact)
                         * (half - max_exact)).astype(i32)
    large = jnp.minimum(large, half - 1)
    return base + jnp.where(nn < max_exact, nn, large)


def _bias_tiles(rel_table):
    t = DSA_T
    dlt = jnp.arange(3, dtype=i32)[:, None, None] - 1
    rel = dlt * t + jnp.arange(t, dtype=i32)[None, :, None] - jnp.arange(t, dtype=i32)[None, None, :]
    onehot = (_t5_bucket(rel)[..., None] == jnp.arange(N_BUCKETS, dtype=i32)).astype(f32)
    tiles = jnp.einsum('dstb,bh->dhst', onehot, rel_table.astype(f32), precision=lax.Precision.HIGHEST)
    far = rel_table.astype(f32)[N_BUCKETS // 2 - 1]
    return (tiles - far[None, :, None, None]) * LOG2E


def _rope_table(n):
    half = A_ROPE // 2
    inv = ROPE_THETA ** (-jnp.arange(half, dtype=f32) / half)
    ang = jnp.arange(n).astype(f32)[:, None] * inv[None, :]
    cos, sin = jnp.cos(ang), jnp.sin(ang)
    return jnp.concatenate([cos, cos, sin, sin], axis=1)


def _split_hi_lo(w):
    hi = w.astype(bf16)
    lo = (w - hi.astype(f32)).astype(bf16)
    return hi, lo


def kernel(x, meta_tokens, rel_bias_table, ev_w_in, mla_q_norm, mla_kv_norm, mla_w_uq, mla_w_ukv,
           ev_w_out, od_w_in, od_w_out, ln_g, ln_b, router_w, router_b, exp_w_in, exp_b_in,
           exp_w_out, exp_b_out):
    bsz, seq, d = x.shape
    depth = ln_g.shape[0]
    alpha = (2 * depth) ** 0.25
    n_tot = seq + N_META
    n = -(-n_tot // Q_BLOCK) * Q_BLOCK
    m = bsz * n
    assert m % TOK_BLK == 0 and d % (2 * LANE) == 0
    h = jnp.concatenate([jnp.broadcast_to(meta_tokens[None].astype(x.dtype), (bsz, N_META, d)), x,
                         jnp.zeros((bsz, n - n_tot, d), x.dtype)], axis=1).reshape(m, d)
    hb = h.astype(bf16)
    k_sel = min(TOPK_MAX, seq // 4)
    tab = _rope_table(n)
    tiles = _bias_tiles(rel_bias_table)
    for i in range(depth):
        j = i // 2
        if i % 2 == 0:
            w_in2, wq2, wkv2 = _prep_even(ev_w_in[j], mla_w_uq[j], mla_w_ukv[j])
            qa, ka, va, sbq, sbk, sbv = _even_proj(hb, w_in2, mla_q_norm[j][None], mla_kv_norm[j][None],
                                                   wq2, wkv2, tab, n)
            out_a = _mla_attention(qa, ka, va, bsz, n)
            out_b = _sb_attention(sbq, sbk, sbv, bsz, n)
            w_o = ev_w_out[j].astype(bf16)
            parts, weights = [out_a, out_b], [w_o[:HA * A_V], w_o[HA * A_V:]]
        else:
            w_main, w_v_t, w_w_t = _prep_odd(od_w_in[j])
            cq, ck, iq, ik, vt, wt = _odd_proj(hb, w_main, w_v_t, w_w_t, n)
            out_c = _dsa_attention(cq, ck, iq, ik, vt, wt, tiles, bsz, n, k_sel)
            parts, weights = [out_c], [od_w_out[j].astype(bf16)]
        rw = jnp.pad(router_w[i], ((0, 0), (0, LANE - N_EXPERTS)))
        rwh, rwl = _split_hi_lo(rw)
        rb = jnp.pad(router_b[i], (0, LANE - N_EXPERTS))[None]
        h1, hp, gate, idx, cnt = _mix_out(parts, weights, h, ln_g[i, 0][None], ln_b[i, 0][None], rwh, rwl, rb, alpha)
        h, hb = _moe(hp, h1, gate, idx, cnt, exp_w_in, exp_b_in[i][:, None, :],
                     exp_w_out, exp_b_out[i][:, None, :], i,
                     ln_g[i, 1][None], ln_b[i, 1][None], alpha)
    return h.reshape(bsz, n, d)[:, N_META:N_META + seq]
```
